```python
import math
import jax, jax.numpy as jnp
from jax import lax
import numpy as np

D_MODEL = 2048
BATCH = 2
SEQ = 8192
DEPTH = 1

HEAD_DIM = 128
DIFF_HEADS = D_MODEL // (2 * HEAD_DIM)
DIFF_V_DIM = D_MODEL // DIFF_HEADS
RET_HEADS = 8
RET_QK_DIM = HEAD_DIM
RET_V_DIM = D_MODEL // RET_HEADS
DIFF_QK_W = DIFF_HEADS * 2 * HEAD_DIM
DIFF_V_W = DIFF_HEADS * DIFF_V_DIM
RET_QK_W = RET_HEADS * RET_QK_DIM
RET_V_W = RET_HEADS * RET_V_DIM
MIX_W = D_MODEL
IN_SIZES = [DIFF_QK_W, DIFF_QK_W, DIFF_V_W,
            RET_QK_W, RET_QK_W, RET_V_W, RET_V_W,
            MIX_W, MIX_W]
IN_W = int(sum(IN_SIZES))
IN_SPLITS = [int(s) for s in np.cumsum(IN_SIZES)[:-1]]
D_FF = -(-8 * D_MODEL // (3 * 256)) * 256
Q_BLOCK = 128
CHUNK = 128
NORM_EPS = 1e-6

kernel_name = "hybrid_diffattn_retention_adaln_block"


def rmsnorm(x, g):
    xf = x.astype(jnp.float32)
    y = xf * lax.rsqrt(jnp.mean(xf * xf, axis=-1, keepdims=True) + NORM_EPS)
    return y.astype(x.dtype) * g


def lambda_init_for_layer(layer_idx):
    return 0.8 - 0.6 * math.exp(-0.3 * layer_idx)


def alibi_slopes(n_heads):
    return jnp.asarray([2.0 ** (-8.0 * (h + 1) / n_heads) for h in range(n_heads)], jnp.float32)


def retention_decays(n_heads):
    return jnp.asarray([1.0 - 2.0 ** (-5 - h) for h in range(n_heads)], jnp.float32)


def diff_attention(q, k, v, lam):
    B, S, H, _, dk = q.shape
    dv = v.shape[-1]
    nb = S // Q_BLOCK
    slopes = alibi_slopes(H)
    scale = dk ** -0.5
    key_pos = jnp.arange(S)
    q_blocks = jnp.moveaxis(q.reshape(B, nb, Q_BLOCK, H, 2, dk), 1, 0)

    def block(args):
        q_blk, i = args
        q_pos = i * Q_BLOCK + jnp.arange(Q_BLOCK)
        dist = (q_pos[:, None] - key_pos[None, :]).astype(jnp.float32)
        causal = dist >= 0
        bias = -slopes[:, None, None] * dist
        s = jnp.einsum('bqhid,bkhid->bhiqk', q_blk, k).astype(jnp.float32) * scale
        s = jnp.where(causal, s + bias[None, :, None], -jnp.inf)
        p = jax.nn.softmax(s, axis=-1)
        a = p[:, :, 0] - lam * p[:, :, 1]
        return jnp.einsum('bhqk,bkhe->bqhe', a.astype(v.dtype), v)

    out = lax.map(block, (q_blocks, jnp.arange(nb)))
    return jnp.moveaxis(out, 0, 1).reshape(B, S, H, dv)


def retention_chunkwise(q, k, v):
    B, S, H, dk = q.shape
    dv = v.shape[-1]
    N = S // CHUNK
    log_g = jnp.log(retention_decays(H))
    q = q.reshape(B, N, CHUNK, H, dk)
    k = (k * (dk ** -0.5)).reshape(B, N, CHUNK, H, dk)
    v = v.reshape(B, N, CHUNK, H, dv)
    idx = jnp.arange(CHUNK, dtype=jnp.float32)
    rel = idx[:, None] - idx[None, :]
    decay = jnp.where(rel >= 0, jnp.exp(log_g[:, None, None] * jnp.maximum(rel, 0.0)), 0.0)
    scores = jnp.einsum('bnchd,bnmhd->bnhcm', q, k) * decay
    intra = jnp.einsum('bnhcm,bnmhe->bnche', scores, v)
    zeta = jnp.exp(log_g[:, None] * (CHUNK - 1 - idx)[None, :])
    kv = jnp.einsum('bnmhd,hm,bnmhe->bnhde', k, zeta, v)
    gamma_chunk = jnp.exp(log_g * CHUNK)[None, :, None, None].astype(kv.dtype)

    def step(R, kv_n):
        return gamma_chunk * R + kv_n, R

    _, R_prev = lax.scan(step, jnp.zeros((B, H, dk, dv), kv.dtype), jnp.moveaxis(kv, 1, 0))
    R_prev = jnp.moveaxis(R_prev, 0, 1)
    xi = jnp.exp(log_g[:, None] * (idx + 1.0)[None, :])
    cross = jnp.einsum('bnchd,bnhde,hc->bnche', q, R_prev, xi)
    return (intra + cross).reshape(B, S, H, dv).astype(v.dtype)


def head_groupnorm(y, w, b):
    B, S, H, dv = y.shape
    yf = y.astype(jnp.float32)
    mu = jnp.mean(yf, axis=-1, keepdims=True)
    var = jnp.mean(jnp.square(yf - mu), axis=-1, keepdims=True)
    yn = ((yf - mu) * lax.rsqrt(var + NORM_EPS)).astype(y.dtype).reshape(B, S, H * dv)
    return yn * w + b


def head_rmsnorm(y, g):
    B, S, H, dv = y.shape
    yf = y.astype(jnp.float32)
    yn = (yf * lax.rsqrt(jnp.mean(yf * yf, axis=-1, keepdims=True) + NORM_EPS)).astype(y.dtype)
    return yn.reshape(B, S, H * dv) * g


def setup_inputs(seed: int = 0) -> dict:
    key = jax.random.key(seed)
    ks = jax.random.split(key, 20)
    f32 = jnp.float32
    L, D = DEPTH, D_MODEL

    def nrm(k, shape, scale):
        return jax.random.normal(k, shape, f32) * scale

    return {
        "x": nrm(ks[0], (BATCH, SEQ, D), 1.0),
        "c": nrm(ks[1], (BATCH, D), 1.0),
        "w_ada": nrm(ks[2], (L, D, 6 * D), 0.5 * D ** -0.5),
        "b_ada": nrm(ks[3], (L, 6 * D), 0.02),
        "norm1_g": 1.0 + nrm(ks[4], (L, D), 0.02),
        "w_in": nrm(ks[5], (L, D, IN_W), D ** -0.5),
        "lam_q1": nrm(ks[6], (L, HEAD_DIM), 0.1),
        "lam_k1": nrm(ks[7], (L, HEAD_DIM), 0.1),
        "lam_q2": nrm(ks[8], (L, HEAD_DIM), 0.1),
        "lam_k2": nrm(ks[9], (L, HEAD_DIM), 0.1),
        "diff_subln_g": 1.0 + nrm(ks[10], (L, DIFF_V_W), 0.02),
        "ret_gn_w": 1.0 + nrm(ks[11], (L, RET_V_W), 0.02),
        "ret_gn_b": nrm(ks[12], (L, RET_V_W), 0.02),
        "w_out": nrm(ks[13], (L, MIX_W, D), MIX_W ** -0.5),
        "norm2_g": 1.0 + nrm(ks[14], (L, D), 0.02),
        "w_ffn_gate": nrm(ks[15], (L, D, D_FF), D ** -0.5),
        "w_ffn_up": nrm(ks[16], (L, D, D_FF), D ** -0.5),
        "w_ffn_down": nrm(ks[17], (L, D_FF, D), D_FF ** -0.5),
        "final_g": 1.0 + nrm(ks[18], (D,), 0.02),
    }


def reference(x, c, w_ada, b_ada, norm1_g, w_in, lam_q1, lam_k1, lam_q2, lam_k2,
              diff_subln_g, ret_gn_w, ret_gn_b, w_out, norm2_g,
              w_ffn_gate, w_ffn_up, w_ffn_down, final_g):
    B, S, D = x.shape
    for l in range(DEPTH):
        mod = jax.nn.silu(c) @ w_ada[l] + b_ada[l]
        shift1, scale1, gate1, shift2, scale2, gate2 = jnp.split(mod, 6, axis=-1)

        h = rmsnorm(x, norm1_g[l]) * (1.0 + scale1[:, None]) + shift1[:, None]
        proj = h @ w_in[l]
        dq, dk_, dv_, rq, rk, rv, rg, ga, gb = jnp.split(proj, IN_SPLITS, axis=-1)

        lam_init = lambda_init_for_layer(l)
        lam = (jnp.exp(jnp.sum(lam_q1[l].astype(jnp.float32) * lam_k1[l].astype(jnp.float32)))
               - jnp.exp(jnp.sum(lam_q2[l].astype(jnp.float32) * lam_k2[l].astype(jnp.float32)))
               + lam_init)
        ya = diff_attention(dq.reshape(B, S, DIFF_HEADS, 2, HEAD_DIM),
                            dk_.reshape(B, S, DIFF_HEADS, 2, HEAD_DIM),
                            dv_.reshape(B, S, DIFF_HEADS, DIFF_V_DIM), lam)
        ya = head_rmsnorm(ya, diff_subln_g[l]) * (1.0 - lam_init)

        yb = retention_chunkwise(rq.reshape(B, S, RET_HEADS, RET_QK_DIM),
                                 rk.reshape(B, S, RET_HEADS, RET_QK_DIM),
                                 rv.reshape(B, S, RET_HEADS, RET_V_DIM))
        yb = jax.nn.silu(rg) * head_groupnorm(yb, ret_gn_w[l], ret_gn_b[l])

        merged = jax.nn.sigmoid(ga) * ya + jax.nn.sigmoid(gb) * yb
        x = x + gate1[:, None] * (merged @ w_out[l])

        h2 = rmsnorm(x, norm2_g[l]) * (1.0 + scale2[:, None]) + shift2[:, None]
        ffn = (jax.nn.silu(h2 @ w_ffn_gate[l]) * (h2 @ w_ffn_up[l])) @ w_ffn_down[l]
        x = x + gate2[:, None] * ffn
    return rmsnorm(x, final_g)
```

```python
import functools
import math

import jax
import jax.numpy as jnp
from jax import lax
from jax.experimental import pallas as pl
from jax.experimental.pallas import tpu as pltpu

F32 = jnp.float32
BF16 = jnp.bfloat16

HEAD_DIM = 128
NORM_EPS = 1e-6
LOG2E = math.log2(math.e)

V7X_VMEM_BYTES = 64 * 1024 * 1024
V7X_SUBLANES = 8

ADA_TN = 1024
INPROJ_TM = 1024
INPROJ_TN = 1024
ATTN_BLOCK = 512
RET_BLOCK = 512
OUTPROJ_TM = 512
FFN_TM = 512
FFN_TF = 512


def _vmem_limit(nbytes):
    return int(min(nbytes + 16 * 1024 * 1024, V7X_VMEM_BYTES - 4 * 1024 * 1024))


def _sigmoid(x):
    return 1.0 / (1.0 + jnp.exp(-x))


def _adaln_kernel(c_ref, w_ref, b_ref, o_ref):
    c = c_ref[...]
    sc = c * _sigmoid(c)
    o_ref[...] = jnp.dot(sc, w_ref[...], preferred_element_type=F32,
                         precision=lax.Precision.HIGHEST) + b_ref[...]


def _adaln(c, w, b):
    bsz, d = c.shape
    n = w.shape[1]
    rows = -(-bsz // V7X_SUBLANES) * V7X_SUBLANES
    c_pad = jnp.pad(c, ((0, rows - bsz), (0, 0)))
    out = pl.pallas_call(
        _adaln_kernel,
        grid=(n // ADA_TN,),
        in_specs=[pl.BlockSpec((rows, d), lambda j: (0, 0)),
                  pl.BlockSpec((d, ADA_TN), lambda j: (0, j)),
                  pl.BlockSpec((1, ADA_TN), lambda j: (0, j))],
        out_specs=pl.BlockSpec((rows, ADA_TN), lambda j: (0, j)),
        out_shape=jax.ShapeDtypeStruct((rows, n), F32),
        compiler_params=pltpu.CompilerParams(
            dimension_semantics=("arbitrary",),
            vmem_limit_bytes=_vmem_limit(2 * d * ADA_TN * 4)),
        name="adaln_mod",
    )(c_pad, w, b.reshape(1, n))
    return out[:bsz]


def _modulated_rmsnorm(x, g, scale, shift):
    ms = jnp.mean(x * x, axis=-1, keepdims=True)
    y = x * lax.rsqrt(ms + NORM_EPS) * g
    return y * (1.0 + scale) + shift


def _inproj_kernel(x_ref, g_ref, scale_ref, shift_ref, w_ref, o_ref, h_ref):
    @pl.when(pl.program_id(2) == 0)
    def _():
        h = _modulated_rmsnorm(x_ref[0], g_ref[...], scale_ref[0], shift_ref[0])
        h_ref[...] = h.astype(BF16)

    o_ref[0] = jnp.dot(h_ref[...], w_ref[...], preferred_element_type=F32).astype(o_ref.dtype)


def _inproj(x, g, scale, shift, w_bf16):
    bsz, s, d = x.shape
    n = w_bf16.shape[1]
    tm, tn = INPROJ_TM, INPROJ_TN
    vmem = 2 * tm * d * 4 + 2 * d * tn * 2 + 2 * tm * tn * 2 + tm * d * 2
    return pl.pallas_call(
        _inproj_kernel,
        grid=(bsz, s // tm, n // tn),
        in_specs=[pl.BlockSpec((1, tm, d), lambda b, i, j: (b, i, 0)),
                  pl.BlockSpec((1, d), lambda b, i, j: (0, 0)),
                  pl.BlockSpec((1, 1, d), lambda b, i, j: (b, 0, 0)),
                  pl.BlockSpec((1, 1, d), lambda b, i, j: (b, 0, 0)),
                  pl.BlockSpec((d, tn), lambda b, i, j: (0, j))],
        out_specs=pl.BlockSpec((1, tm, tn), lambda b, i, j: (b, i, j)),
        out_shape=jax.ShapeDtypeStruct((bsz, s, n), BF16),
        scratch_shapes=[pltpu.VMEM((tm, d), BF16)],
        compiler_params=pltpu.CompilerParams(
            dimension_semantics=("arbitrary", "arbitrary", "arbitrary"),
            vmem_limit_bytes=_vmem_limit(vmem)),
        name="inproj",
    )(x, g.reshape(1, d), scale, shift, w_bf16)


def _diff_attn_kernel(slopes_ref, q1_ref, q2_ref, k1_ref, k2_ref, v_ref,
                      lq1_ref, lk1_ref, lq2_ref, lk2_ref, g_ref, o_ref,
                      bias_ref, bias_diag_ref, acc1_ref, acc2_ref, *, lam_init, blk):
    h = pl.program_id(1)
    i = pl.program_id(2)
    slope_l2 = slopes_ref[h] * LOG2E

    @pl.when(i == 0)
    def _():
        row = lax.broadcasted_iota(jnp.int32, (blk, blk), 0)
        col = lax.broadcasted_iota(jnp.int32, (blk, blk), 1)
        bias = (col - row).astype(F32) * slope_l2
        bias_ref[...] = bias
        bias_diag_ref[...] = jnp.where(row >= col, bias, -jnp.inf)

    qscale = (HEAD_DIM ** -0.5) * LOG2E
    q1 = (q1_ref[0].astype(F32) * qscale).astype(BF16)
    q2 = (q2_ref[0].astype(F32) * qscale).astype(BF16)

    acc1_ref[...] = jnp.zeros_like(acc1_ref)
    acc2_ref[...] = jnp.zeros_like(acc2_ref)

    def one_map(q, k_ref, acc_ref, start, bias, off, m_old, l_old, v):
        k = k_ref[0, pl.ds(start, blk), :]
        s = lax.dot_general(q, k, (((1,), (1,)), ((), ())), preferred_element_type=F32)
        t = s + bias
        m_new = jnp.maximum(m_old, jnp.max(t, axis=-1, keepdims=True) - off)
        p = jnp.exp2(t - (m_new + off))
        alpha = jnp.exp2(m_old - m_new)
        l_new = alpha * l_old + jnp.sum(p, axis=-1, keepdims=True)
        acc_ref[...] = alpha * acc_ref[...] + jnp.dot(p.astype(BF16), v, preferred_element_type=F32)
        return m_new, l_new

    def tile(j, carry, bias):
        m1, l1, m2, l2 = carry
        start = pl.multiple_of(j * blk, blk)
        off = slope_l2 * ((i - j) * blk).astype(F32)
        v = v_ref[0, pl.ds(start, blk), :]
        m1, l1 = one_map(q1, k1_ref, acc1_ref, start, bias, off, m1, l1, v)
        m2, l2 = one_map(q2, k2_ref, acc2_ref, start, bias, off, m2, l2, v)
        return m1, l1, m2, l2

    neg = jnp.full((blk, 1), -jnp.inf, F32)
    zero = jnp.zeros((blk, 1), F32)
    carry = lax.fori_loop(0, i, lambda j, c: tile(j, c, bias_ref[...]), (neg, zero, neg, zero))
    m1, l1, m2, l2 = tile(i, carry, bias_diag_ref[...])

    lam = (jnp.exp(jnp.sum(lq1_ref[...] * lk1_ref[...], axis=-1, keepdims=True))
           - jnp.exp(jnp.sum(lq2_ref[...] * lk2_ref[...], axis=-1, keepdims=True))
           + lam_init)
    y = acc1_ref[...] / l1 - lam * (acc2_ref[...] / l2)
    yn = y * lax.rsqrt(jnp.mean(y * y, axis=-1, keepdims=True) + NORM_EPS)
    o_ref[0] = (yn * g_ref[...] * (1.0 - lam_init)).astype(o_ref.dtype)


def _diff_attention(proj, slopes, lq1, lk1, lq2, lk2, subln_g, *, n_heads, d_model, lam_init):
    bsz, s, _ = proj.shape
    blk = ATTN_BLOCK
    dv = d_model // n_heads
    k_col0 = (n_heads * 2 * HEAD_DIM) // HEAD_DIM
    v_col0 = (2 * n_heads * 2 * HEAD_DIM) // dv
    qspec = lambda m: pl.BlockSpec((1, blk, HEAD_DIM), lambda b, h, i, sl: (b, i, 2 * h + m))
    kspec = lambda m: pl.BlockSpec((1, s, HEAD_DIM), lambda b, h, i, sl: (b, 0, k_col0 + 2 * h + m))
    vec = pl.BlockSpec((1, HEAD_DIM), lambda b, h, i, sl: (0, 0))
    vmem = (2 * 2 * s * HEAD_DIM * 2 * 2 + 2 * s * dv * 2 + 2 * blk * blk * 4 + 2 * blk * dv * 4
            + 6 * blk * blk * 4)
    kernel = functools.partial(_diff_attn_kernel, lam_init=lam_init, blk=blk)
    return pl.pallas_call(
        kernel,
        grid_spec=pltpu.PrefetchScalarGridSpec(
            num_scalar_prefetch=1,
            grid=(bsz, n_heads, s // blk),
            in_specs=[qspec(0), qspec(1), kspec(0), kspec(1),
                      pl.BlockSpec((1, s, dv), lambda b, h, i, sl: (b, 0, v_col0 + h)),
                      vec, vec, vec, vec,
                      pl.BlockSpec((1, dv), lambda b, h, i, sl: (0, h))],
            out_specs=pl.BlockSpec((1, blk, dv), lambda b, h, i, sl: (b, i, h)),
            scratch_shapes=[pltpu.VMEM((blk, blk), F32), pltpu.VMEM((blk, blk), F32),
                            pltpu.VMEM((blk, dv), F32), pltpu.VMEM((blk, dv), F32)]),
        out_shape=jax.ShapeDtypeStruct((bsz, s, d_model), BF16),
        compiler_params=pltpu.CompilerParams(
            dimension_semantics=("arbitrary", "arbitrary", "arbitrary"),
            vmem_limit_bytes=_vmem_limit(vmem)),
        name="diff_attention",
    )(slopes, proj, proj, proj, proj, proj, lq1, lk1, lq2, lk2, subln_g.reshape(1, d_model))


def _retention_merge_kernel(logg_ref, q_ref, k_ref, v_ref, rg_ref, ga_ref, gb_ref, ya_ref,
                            gnw_ref, gnb_ref, o_ref, decay_ref, state_ref, *, blk):
    h = pl.program_id(1)
    n = pl.program_id(2)
    log_g = logg_ref[h]
    kscale = HEAD_DIM ** -0.5

    @pl.when(n == 0)
    def _():
        row = lax.broadcasted_iota(jnp.int32, (blk, blk), 0)
        col = lax.broadcasted_iota(jnp.int32, (blk, blk), 1)
        rel = (row - col).astype(F32)
        decay_ref[...] = jnp.where(rel >= 0, jnp.exp(log_g * jnp.maximum(rel, 0.0)) * kscale, 0.0)
        state_ref[...] = jnp.zeros_like(state_ref)

    q = q_ref[0]
    k = k_ref[0]
    v = v_ref[0]
    idx = lax.broadcasted_iota(jnp.int32, (blk, 1), 0).astype(F32)
    xi = jnp.exp(log_g * (idx + 1.0))
    zeta = jnp.exp(log_g * (blk - 1.0 - idx)) * kscale

    s = lax.dot_general(q, k, (((1,), (1,)), ((), ())), preferred_element_type=F32)
    intra = jnp.dot((s * decay_ref[...]).astype(BF16), v, preferred_element_type=F32)
    state = state_ref[...]
    cross = jnp.dot((q.astype(F32) * xi).astype(BF16), state.astype(BF16), preferred_element_type=F32)
    y = intra + cross

    kz = (k.astype(F32) * zeta).astype(BF16)
    kv = lax.dot_general(kz, v, (((0,), (0,)), ((), ())), preferred_element_type=F32)
    state_ref[...] = jnp.exp(log_g * blk) * state + kv

    mu = jnp.mean(y, axis=-1, keepdims=True)
    yc = y - mu
    var = jnp.mean(yc * yc, axis=-1, keepdims=True)
    gn = yc * lax.rsqrt(var + NORM_EPS) * gnw_ref[...] + gnb_ref[...]
    rg = rg_ref[0].astype(F32)
    yb = rg * _sigmoid(rg) * gn
    merged = (_sigmoid(ga_ref[0].astype(F32)) * ya_ref[0].astype(F32)
              + _sigmoid(gb_ref[0].astype(F32)) * yb)
    o_ref[0] = merged.astype(o_ref.dtype)


def _retention_merge(proj, ya, log_g, gn_w, gn_b, *, n_heads, d_model, col0):
    bsz, s, _ = proj.shape
    blk = RET_BLOCK
    dv = d_model // n_heads
    q0 = col0 // HEAD_DIM
    k0 = q0 + n_heads
    v0 = (col0 + 2 * n_heads * HEAD_DIM) // dv
    rg0 = v0 + n_heads
    ga0 = rg0 + n_heads
    gb0 = ga0 + n_heads
    wide = lambda c0: pl.BlockSpec((1, blk, dv), lambda b, h, n, lg: (b, n, c0 + h))
    narrow = lambda c0: pl.BlockSpec((1, blk, HEAD_DIM), lambda b, h, n, lg: (b, n, c0 + h))
    chan = pl.BlockSpec((1, dv), lambda b, h, n, lg: (0, h))
    vmem = 2 * (2 * blk * HEAD_DIM * 2 + 6 * blk * dv * 2) + blk * blk * 4 + HEAD_DIM * dv * 4 \
        + 4 * blk * blk * 4
    kernel = functools.partial(_retention_merge_kernel, blk=blk)
    return pl.pallas_call(
        kernel,
        grid_spec=pltpu.PrefetchScalarGridSpec(
            num_scalar_prefetch=1,
            grid=(bsz, n_heads, s // blk),
            in_specs=[narrow(q0), narrow(k0), wide(v0), wide(rg0), wide(ga0), wide(gb0),
                      wide(0), chan, chan],
            out_specs=wide(0),
            scratch_shapes=[pltpu.VMEM((blk, blk), F32), pltpu.VMEM((HEAD_DIM, dv), F32)]),
        out_shape=jax.ShapeDtypeStruct((bsz, s, d_model), BF16),
        compiler_params=pltpu.CompilerParams(
            dimension_semantics=("arbitrary", "arbitrary", "arbitrary"),
            vmem_limit_bytes=_vmem_limit(vmem)),
        name="retention_merge",
    )(log_g, proj, proj, proj, proj, proj, proj, ya, gn_w.reshape(1, d_model), gn_b.reshape(1, d_model))


def _outproj_kernel(m_ref, w_ref, x_ref, gate_ref, g_ref, scale_ref, shift_ref, x1_ref, h2_ref):
    y = jnp.dot(m_ref[0], w_ref[...], preferred_element_type=F32)
    x1 = x_ref[0] + gate_ref[0] * y
    x1_ref[0] = x1
    h2_ref[0] = _modulated_rmsnorm(x1, g_ref[...], scale_ref[0], shift_ref[0]).astype(h2_ref.dtype)


def _outproj(merged, w_bf16, x, gate, g, scale, shift):
    bsz, s, d = x.shape
    tm = OUTPROJ_TM
    rows = pl.BlockSpec((1, tm, d), lambda b, i: (b, i, 0))
    mod = pl.BlockSpec((1, 1, d), lambda b, i: (b, 0, 0))
    vmem = 2 * (tm * d * 2 + d * d * 2 + tm * d * 4 + tm * d * 4 + tm * d * 2)
    return pl.pallas_call(
        _outproj_kernel,
        grid=(bsz, s // tm),
        in_specs=[rows, pl.BlockSpec((d, d), lambda b, i: (0, 0)), rows, mod,
                  pl.BlockSpec((1, d), lambda b, i: (0, 0)), mod, mod],
        out_specs=[rows, rows],
        out_shape=[jax.ShapeDtypeStruct((bsz, s, d), F32), jax.ShapeDtypeStruct((bsz, s, d), BF16)],
        compiler_params=pltpu.CompilerParams(
            dimension_semantics=("arbitrary", "arbitrary"),
            vmem_limit_bytes=_vmem_limit(vmem)),
        name="outproj",
    )(merged, w_bf16, x, gate, g.reshape(1, d), scale, shift)


def _ffn_kernel(h_ref, wg_ref, wu_ref, wd_ref, x1_ref, gate_ref, fg_ref, o_ref, acc_ref):
    f = pl.program_id(2)

    @pl.when(f == 0)
    def _():
        acc_ref[...] = jnp.zeros_like(acc_ref)

    h = h_ref[0]
    g = jnp.dot(h, wg_ref[...], preferred_element_type=F32)
    u = jnp.dot(h, wu_ref[...], preferred_element_type=F32)
    a = (g * _sigmoid(g) * u).astype(BF16)
    acc_ref[...] += jnp.dot(a, wd_ref[...], preferred_element_type=F32)

    @pl.when(f == pl.num_programs(2) - 1)
    def _():
        x2 = x1_ref[0] + gate_ref[0] * acc_ref[...]
        ms = jnp.mean(x2 * x2, axis=-1, keepdims=True)
        o_ref[0] = x2 * lax.rsqrt(ms + NORM_EPS) * fg_ref[...]


def _ffn(h2, wg, wu, wd, x1, gate, final_g):
    bsz, s, d = x1.shape
    dff = wg.shape[1]
    tm, tf = FFN_TM, FFN_TF
    rows = lambda: pl.BlockSpec((1, tm, d), lambda b, i, f: (b, i, 0))
    vmem = 2 * (tm * d * 2 + 2 * d * tf * 2 + tf * d * 2 + tm * d * 4 + tm * d * 4) + tm * d * 4
    return pl.pallas_call(
        _ffn_kernel,
        grid=(bsz, s // tm, dff // tf),
        in_specs=[rows(),
                  pl.BlockSpec((d, tf), lambda b, i, f: (0, f)),
                  pl.BlockSpec((d, tf), lambda b, i, f: (0, f)),
                  pl.BlockSpec((tf, d), lambda b, i, f: (f, 0)),
                  rows(),
                  pl.BlockSpec((1, 1, d), lambda b, i, f: (b, 0, 0)),
                  pl.BlockSpec((1, d), lambda b, i, f: (0, 0))],
        out_specs=rows(),
        out_shape=jax.ShapeDtypeStruct((bsz, s, d), F32),
        scratch_shapes=[pltpu.VMEM((tm, d), F32)],
        compiler_params=pltpu.CompilerParams(
            dimension_semantics=("arbitrary", "arbitrary", "arbitrary"),
            vmem_limit_bytes=_vmem_limit(vmem)),
        name="ffn",
    )(h2, wg, wu, wd, x1, gate, final_g.reshape(1, d))


def kernel(x, c, w_ada, b_ada, norm1_g, w_in, lam_q1, lam_k1, lam_q2, lam_k2, diff_subln_g,
           ret_gn_w, ret_gn_b, w_out, norm2_g, w_ffn_gate, w_ffn_up, w_ffn_down, final_g):
    bsz, s, d = x.shape
    depth = w_ada.shape[0]
    assert depth == 1, "the final RMSNorm is fused into the FFN kernel of the only layer"
    n_heads = d // (2 * HEAD_DIM)
    diff_qk_w = n_heads * 2 * HEAD_DIM
    ret_col0 = 2 * diff_qk_w + d
    slopes = jnp.asarray([2.0 ** (-8.0 * (h + 1) / n_heads) for h in range(n_heads)], F32)
    log_g = jnp.asarray([math.log(1.0 - 2.0 ** (-5 - h)) for h in range(n_heads)], F32)

    for l in range(depth):
        lam_init = 0.8 - 0.6 * math.exp(-0.3 * l)
        mod = _adaln(c, w_ada[l], b_ada[l])
        shift1, scale1, gate1, shift2, scale2, gate2 = (
            m.reshape(bsz, 1, d) for m in jnp.split(mod, 6, axis=-1))
        proj = _inproj(x, norm1_g[l], scale1, shift1, w_in[l].astype(BF16))
        ya = _diff_attention(proj, slopes, lam_q1[l:l + 1], lam_k1[l:l + 1], lam_q2[l:l + 1],
                             lam_k2[l:l + 1], diff_subln_g[l], n_heads=n_heads, d_model=d,
                             lam_init=lam_init)
        merged = _retention_merge(proj, ya, log_g, ret_gn_w[l], ret_gn_b[l],
                                  n_heads=n_heads, d_model=d, col0=ret_col0)
        x1, h2 = _outproj(merged, w_out[l].astype(BF16), x, gate1, norm2_g[l], scale2, shift2)
        x = _ffn(h2, w_ffn_gate[l].astype(BF16), w_ffn_up[l].astype(BF16),
                 w_ffn_down[l].astype(BF16), x1, gate2, final_g)
    return x
```

```python
import functools
import math

import jax
import jax.numpy as jnp
from jax import lax
from jax.experimental import pallas as pl
from jax.experimental.pallas import tpu as pltpu

F32 = jnp.float32
BF16 = jnp.bfloat16

HEAD_DIM = 128
NORM_EPS = 1e-6
LOG2E = math.log2(math.e)

V7X_VMEM_BYTES = 64 * 1024 * 1024
V7X_SUBLANES = 8

ADA_TN = 1024
INPROJ_TM = 1024
INPROJ_TN = 1024
ATTN_BLOCK = 512
ATTN_STRIP = 256
RET_BLOCK = 512
OUTPROJ_TM = 512
FFN_TM = 512
FFN_TF = 512


def _vmem_limit(nbytes):
    return int(min(nbytes + 16 * 1024 * 1024, V7X_VMEM_BYTES - 4 * 1024 * 1024))


def _sigmoid(x):
    return 1.0 / (1.0 + jnp.exp(-x))


def _adaln_kernel(c_ref, w_ref, b_ref, o_ref):
    c = c_ref[...]
    sc = c * _sigmoid(c)
    o_ref[...] = jnp.dot(sc, w_ref[...], preferred_element_type=F32,
                         precision=lax.Precision.HIGHEST) + b_ref[...]


def _adaln(c, w, b):
    bsz, d = c.shape
    n = w.shape[1]
    rows = -(-bsz // V7X_SUBLANES) * V7X_SUBLANES
    c_pad = jnp.pad(c, ((0, rows - bsz), (0, 0)))
    out = pl.pallas_call(
        _adaln_kernel,
        grid=(n // ADA_TN,),
        in_specs=[pl.BlockSpec((rows, d), lambda j: (0, 0)),
                  pl.BlockSpec((d, ADA_TN), lambda j: (0, j)),
                  pl.BlockSpec((1, ADA_TN), lambda j: (0, j))],
        out_specs=pl.BlockSpec((rows, ADA_TN), lambda j: (0, j)),
        out_shape=jax.ShapeDtypeStruct((rows, n), F32),
        compiler_params=pltpu.CompilerParams(
            dimension_semantics=("arbitrary",),
            vmem_limit_bytes=_vmem_limit(2 * d * ADA_TN * 4)),
        name="adaln_mod",
    )(c_pad, w, b.reshape(1, n))
    return out[:bsz]


def _modulated_rmsnorm(x, g, scale, shift):
    ms = jnp.mean(x * x, axis=-1, keepdims=True)
    y = x * lax.rsqrt(ms + NORM_EPS) * g
    return y * (1.0 + scale) + shift


def _inproj_kernel(x_ref, g_ref, scale_ref, shift_ref, w_ref, o_ref, h_ref):
    @pl.when(pl.program_id(2) == 0)
    def _():
        h = _modulated_rmsnorm(x_ref[0], g_ref[...], scale_ref[0], shift_ref[0])
        h_ref[...] = h.astype(BF16)

    o_ref[0] = jnp.dot(h_ref[...], w_ref[...], preferred_element_type=F32).astype(o_ref.dtype)


def _inproj(x, g, scale, shift, w_bf16):
    bsz, s, d = x.shape
    n = w_bf16.shape[1]
    tm, tn = INPROJ_TM, INPROJ_TN
    vmem = 2 * tm * d * 4 + 2 * d * tn * 2 + 2 * tm * tn * 2 + tm * d * 2
    return pl.pallas_call(
        _inproj_kernel,
        grid=(bsz, s // tm, n // tn),
        in_specs=[pl.BlockSpec((1, tm, d), lambda b, i, j: (b, i, 0)),
                  pl.BlockSpec((1, d), lambda b, i, j: (0, 0)),
                  pl.BlockSpec((1, 1, d), lambda b, i, j: (b, 0, 0)),
                  pl.BlockSpec((1, 1, d), lambda b, i, j: (b, 0, 0)),
                  pl.BlockSpec((d, tn), lambda b, i, j: (0, j))],
        out_specs=pl.BlockSpec((1, tm, tn), lambda b, i, j: (b, i, j)),
        out_shape=jax.ShapeDtypeStruct((bsz, s, n), BF16),
        scratch_shapes=[pltpu.VMEM((tm, d), BF16)],
        compiler_params=pltpu.CompilerParams(
            dimension_semantics=("arbitrary", "arbitrary", "arbitrary"),
            vmem_limit_bytes=_vmem_limit(vmem)),
        name="inproj",
    )(x, g.reshape(1, d), scale, shift, w_bf16)


def _diff_attn_kernel(slopes_ref, q1_ref, q2_ref, k1_ref, k2_ref, v_ref,
                      lq1_ref, lk1_ref, lq2_ref, lk2_ref, g_ref, o_ref,
                      bias_ref, vt_ref, acc1_ref, acc2_ref, p1_ref, p2_ref, *, lam_init, blk):
    h = pl.program_id(1)
    i = pl.program_id(2)
    slope_l2 = slopes_ref[h] * LOG2E
    n_kv = v_ref.shape[1] // blk

    @pl.when(i == 0)
    def _():
        key = lax.broadcasted_iota(jnp.int32, (blk, blk), 0)
        qry = lax.broadcasted_iota(jnp.int32, (blk, blk), 1)
        bias = (key - qry).astype(F32) * slope_l2
        bias_ref[0] = bias
        bias_ref[1] = jnp.where(qry >= key, bias, -jnp.inf)

        def xpose(c, _):
            start = pl.multiple_of(c * blk, blk)
            vt_ref[:, pl.ds(start, blk)] = v_ref[0, pl.ds(start, blk), :].astype(F32).T.astype(BF16)
            return 0
        lax.fori_loop(0, n_kv, xpose, 0)

    qscale = (HEAD_DIM ** -0.5) * LOG2E
    q1t = (q1_ref[0].astype(F32) * qscale).T.astype(BF16)
    q2t = (q2_ref[0].astype(F32) * qscale).T.astype(BF16)

    acc1_ref[...] = jnp.zeros_like(acc1_ref)
    acc2_ref[...] = jnp.zeros_like(acc2_ref)
    p1_ref[1] = jnp.zeros((blk, blk), BF16)
    p2_ref[1] = jnp.zeros((blk, blk), BF16)

    def pv_stage(vt, acc_ref, p_ref, slot, alpha):
        acc_ref[...] = alpha * acc_ref[...] + jnp.dot(vt, p_ref[slot], preferred_element_type=F32)

    def map_stage(qt, k_ref, acc_ref, p_ref, slot, start, vt_prev, bias, off, m_old, l_old, a_prev):
        k = k_ref[0, pl.ds(start, blk), :]
        strips = [slice(c, c + ATTN_STRIP) for c in range(0, blk, ATTN_STRIP)]
        ts = [jnp.dot(k, qt[:, cs], preferred_element_type=F32) for cs in strips]
        pv_stage(vt_prev, acc_ref, p_ref, 1 - slot, a_prev)
        m_out, l_out, a_out = [], [], []
        for cs, s in zip(strips, ts):
            t = s + bias[:, cs]
            m_new = jnp.maximum(m_old[:, cs], jnp.max(t, axis=0, keepdims=True) - off)
            p = jnp.exp2(t - (m_new + off))
            alpha = jnp.exp2(m_old[:, cs] - m_new)
            l_out.append(alpha * l_old[:, cs] + jnp.sum(p, axis=0, keepdims=True))
            m_out.append(m_new)
            a_out.append(alpha)
            p_ref[slot, :, cs] = p.astype(BF16)
        cat = lambda xs: jnp.concatenate(xs, axis=1)
        return cat(m_out), cat(l_out), cat(a_out)

    def step(j, carry, bias):
        m1, l1, a1, m2, l2, a2 = carry
        slot = j % 2
        start = pl.multiple_of(j * blk, blk)
        prev = pl.multiple_of(jnp.maximum(j - 1, 0) * blk, blk)
        vt_prev = vt_ref[:, pl.ds(prev, blk)]
        off = slope_l2 * ((i - j) * blk).astype(F32)
        m1, l1, a1 = map_stage(q1t, k1_ref, acc1_ref, p1_ref, slot, start, vt_prev, bias, off, m1, l1, a1)
        m2, l2, a2 = map_stage(q2t, k2_ref, acc2_ref, p2_ref, slot, start, vt_prev, bias, off, m2, l2, a2)
        return m1, l1, a1, m2, l2, a2

    neg = jnp.full((1, blk), -jnp.inf, F32)
    zero = jnp.zeros((1, blk), F32)
    one = jnp.ones((1, blk), F32)
    carry = lax.fori_loop(0, i, lambda j, c: step(j, c, bias_ref[0]), (neg, zero, one, neg, zero, one))
    m1, l1, a1, m2, l2, a2 = step(i, carry, bias_ref[1])
    vt_last = vt_ref[:, pl.ds(pl.multiple_of(i * blk, blk), blk)]
    pv_stage(vt_last, acc1_ref, p1_ref, i % 2, a1)
    pv_stage(vt_last, acc2_ref, p2_ref, i % 2, a2)

    lam = (jnp.exp(jnp.sum(lq1_ref[...] * lk1_ref[...], axis=-1, keepdims=True))
           - jnp.exp(jnp.sum(lq2_ref[...] * lk2_ref[...], axis=-1, keepdims=True))
           + lam_init)
    y = (acc1_ref[...] / l1 - lam * (acc2_ref[...] / l2)).T
    yn = y * lax.rsqrt(jnp.mean(y * y, axis=-1, keepdims=True) + NORM_EPS)
    o_ref[0] = (yn * g_ref[...] * (1.0 - lam_init)).astype(o_ref.dtype)


def _diff_attention(proj, slopes, lq1, lk1, lq2, lk2, subln_g, *, n_heads, d_model, lam_init):
    bsz, s, _ = proj.shape
    blk = ATTN_BLOCK
    dv = d_model // n_heads
    k_col0 = (n_heads * 2 * HEAD_DIM) // HEAD_DIM
    v_col0 = (2 * n_heads * 2 * HEAD_DIM) // dv
    qspec = lambda m: pl.BlockSpec((1, blk, HEAD_DIM), lambda b, h, i, sl: (b, i, 2 * h + m))
    kspec = lambda m: pl.BlockSpec((1, s, HEAD_DIM), lambda b, h, i, sl: (b, 0, k_col0 + 2 * h + m))
    vec = pl.BlockSpec((1, HEAD_DIM), lambda b, h, i, sl: (0, 0))
    vmem = (2 * 2 * s * HEAD_DIM * 2 + 3 * s * dv * 2 + 2 * blk * blk * 4 + 2 * blk * dv * 4
            + 6 * blk * blk * 4)
    kernel = functools.partial(_diff_attn_kernel, lam_init=lam_init, blk=blk)
    return pl.pallas_call(
        kernel,
        grid_spec=pltpu.PrefetchScalarGridSpec(
            num_scalar_prefetch=1,
            grid=(bsz, n_heads, s // blk),
            in_specs=[qspec(0), qspec(1), kspec(0), kspec(1),
                      pl.BlockSpec((1, s, dv), lambda b, h, i, sl: (b, 0, v_col0 + h)),
                      vec, vec, vec, vec,
                      pl.BlockSpec((1, dv), lambda b, h, i, sl: (0, h))],
            out_specs=pl.BlockSpec((1, blk, dv), lambda b, h, i, sl: (b, i, h)),
            scratch_shapes=[pltpu.VMEM((2, blk, blk), F32),
                            pltpu.VMEM((dv, s), BF16),
                            pltpu.VMEM((dv, blk), F32), pltpu.VMEM((dv, blk), F32),
                            pltpu.VMEM((2, blk, blk), BF16), pltpu.VMEM((2, blk, blk), BF16)]),
        out_shape=jax.ShapeDtypeStruct((bsz, s, d_model), BF16),
        compiler_params=pltpu.CompilerParams(
            dimension_semantics=("arbitrary", "arbitrary", "arbitrary"),
            vmem_limit_bytes=_vmem_limit(vmem)),
        name="diff_attention",
    )(slopes, proj, proj, proj, proj, proj, lq1, lk1, lq2, lk2, subln_g.reshape(1, d_model))


def _retention_merge_kernel(logg_ref, q_ref, k_ref, v_ref, rg_ref, ga_ref, gb_ref, ya_ref,
                            gnw_ref, gnb_ref, o_ref, decay_ref, state_ref, *, blk):
    h = pl.program_id(1)
    n = pl.program_id(2)
    log_g = logg_ref[h]
    kscale = HEAD_DIM ** -0.5

    @pl.when(n == 0)
    def _():
        row = lax.broadcasted_iota(jnp.int32, (blk, blk), 0)
        col = lax.broadcasted_iota(jnp.int32, (blk, blk), 1)
        rel = (row - col).astype(F32)
        decay_ref[...] = jnp.where(rel >= 0, jnp.exp(log_g * jnp.maximum(rel, 0.0)) * kscale, 0.0)
        state_ref[...] = jnp.zeros_like(state_ref)

    q = q_ref[0]
    k = k_ref[0]
    v = v_ref[0]
    idx = lax.broadcasted_iota(jnp.int32, (blk, 1), 0).astype(F32)
    xi = jnp.exp(log_g * (idx + 1.0))
    zeta = jnp.exp(log_g * (blk - 1.0 - idx)) * kscale

    s = lax.dot_general(q, k, (((1,), (1,)), ((), ())), preferred_element_type=F32)
    intra = jnp.dot((s * decay_ref[...]).astype(BF16), v, preferred_element_type=F32)
    state = state_ref[...]
    cross = jnp.dot((q.astype(F32) * xi).astype(BF16), state.astype(BF16), preferred_element_type=F32)
    y = intra + cross

    kz = (k.astype(F32) * zeta).astype(BF16)
    kv = lax.dot_general(kz, v, (((0,), (0,)), ((), ())), preferred_element_type=F32)
    state_ref[...] = jnp.exp(log_g * blk) * state + kv

    mu = jnp.mean(y, axis=-1, keepdims=True)
    yc = y - mu
    var = jnp.mean(yc * yc, axis=-1, keepdims=True)
    gn = yc * lax.rsqrt(var + NORM_EPS) * gnw_ref[...] + gnb_ref[...]
    rg = rg_ref[0].astype(F32)
    yb = rg * _sigmoid(rg) * gn
    merged = (_sigmoid(ga_ref[0].astype(F32)) * ya_ref[0].astype(F32)
              + _sigmoid(gb_ref[0].astype(F32)) * yb)
    o_ref[0] = merged.astype(o_ref.dtype)


def _retention_merge(proj, ya, log_g, gn_w, gn_b, *, n_heads, d_model, col0):
    bsz, s, _ = proj.shape
    blk = RET_BLOCK
    dv = d_model // n_heads
    q0 = col0 // HEAD_DIM
    k0 = q0 + n_heads
    v0 = (col0 + 2 * n_heads * HEAD_DIM) // dv
    rg0 = v0 + n_heads
    ga0 = rg0 + n_heads
    gb0 = ga0 + n_heads
    wide = lambda c0: pl.BlockSpec((1, blk, dv), lambda b, h, n, lg: (b, n, c0 + h))
    narrow = lambda c0: pl.BlockSpec((1, blk, HEAD_DIM), lambda b, h, n, lg: (b, n, c0 + h))
    chan = pl.BlockSpec((1, dv), lambda b, h, n, lg: (0, h))
    vmem = 2 * (2 * blk * HEAD_DIM * 2 + 6 * blk * dv * 2) + blk * blk * 4 + HEAD_DIM * dv * 4 \
        + 4 * blk * blk * 4
    kernel = functools.partial(_retention_merge_kernel, blk=blk)
    return pl.pallas_call(
        kernel,
        grid_spec=pltpu.PrefetchScalarGridSpec(
            num_scalar_prefetch=1,
            grid=(bsz, n_heads, s // blk),
            in_specs=[narrow(q0), narrow(k0), wide(v0), wide(rg0), wide(ga0), wide(gb0),
                      wide(0), chan, chan],
            out_specs=wide(0),
            scratch_shapes=[pltpu.VMEM((blk, blk), F32), pltpu.VMEM((HEAD_DIM, dv), F32)]),
        out_shape=jax.ShapeDtypeStruct((bsz, s, d_model), BF16),
        compiler_params=pltpu.CompilerParams(
            dimension_semantics=("arbitrary", "arbitrary", "arbitrary"),
            vmem_limit_bytes=_vmem_limit(vmem)),
        name="retention_merge",
    )(log_g, proj, proj, proj, proj, proj, proj, ya, gn_w.reshape(1, d_model), gn_b.reshape(1, d_model))


def _outproj_kernel(m_ref, w_ref, x_ref, gate_ref, g_ref, scale_ref, shift_ref, x1_ref, h2_ref):
    y = jnp.dot(m_ref[0], w_ref[...], preferred_element_type=F32)
    x1 = x_ref[0] + gate_ref[0] * y
    x1_ref[0] = x1
    h2_ref[0] = _modulated_rmsnorm(x1, g_ref[...], scale_ref[0], shift_ref[0]).astype(h2_ref.dtype)


def _outproj(merged, w_bf16, x, gate, g, scale, shift):
    bsz, s, d = x.shape
    tm = OUTPROJ_TM
    rows = pl.BlockSpec((1, tm, d), lambda b, i: (b, i, 0))
    mod = pl.BlockSpec((1, 1, d), lambda b, i: (b, 0, 0))
    vmem = 2 * (tm * d * 2 + d * d * 2 + tm * d * 4 + tm * d * 4 + tm * d * 2)
    return pl.pallas_call(
        _outproj_kernel,
        grid=(bsz, s // tm),
        in_specs=[rows, pl.BlockSpec((d, d), lambda b, i: (0, 0)), rows, mod,
                  pl.BlockSpec((1, d), lambda b, i: (0, 0)), mod, mod],
        out_specs=[rows, rows],
        out_shape=[jax.ShapeDtypeStruct((bsz, s, d), F32), jax.ShapeDtypeStruct((bsz, s, d), BF16)],
        compiler_params=pltpu.CompilerParams(
            dimension_semantics=("arbitrary", "arbitrary"),
            vmem_limit_bytes=_vmem_limit(vmem)),
        name="outproj",
    )(merged, w_bf16, x, gate, g.reshape(1, d), scale, shift)


def _ffn_kernel(h_ref, wg_ref, wu_ref, wd_ref, x1_ref, gate_ref, fg_ref, o_ref, acc_ref):
    f = pl.program_id(2)

    @pl.when(f == 0)
    def _():
        acc_ref[...] = jnp.zeros_like(acc_ref)

    h = h_ref[0]
    g = jnp.dot(h, wg_ref[...], preferred_element_type=F32)
    u = jnp.dot(h, wu_ref[...], preferred_element_type=F32)
    a = (g * _sigmoid(g) * u).astype(BF16)
    acc_ref[...] += jnp.dot(a, wd_ref[...], preferred_element_type=F32)

    @pl.when(f == pl.num_programs(2) - 1)
    def _():
        x2 = x1_ref[0] + gate_ref[0] * acc_ref[...]
        ms = jnp.mean(x2 * x2, axis=-1, keepdims=True)
        o_ref[0] = x2 * lax.rsqrt(ms + NORM_EPS) * fg_ref[...]


def _ffn(h2, wg, wu, wd, x1, gate, final_g):
    bsz, s, d = x1.shape
    dff = wg.shape[1]
    tm, tf = FFN_TM, FFN_TF
    rows = lambda: pl.BlockSpec((1, tm, d), lambda b, i, f: (b, i, 0))
    vmem = 2 * (tm * d * 2 + 2 * d * tf * 2 + tf * d * 2 + tm * d * 4 + tm * d * 4) + tm * d * 4
    return pl.pallas_call(
        _ffn_kernel,
        grid=(bsz, s // tm, dff // tf),
        in_specs=[rows(),
                  pl.BlockSpec((d, tf), lambda b, i, f: (0, f)),
                  pl.BlockSpec((d, tf), lambda b, i, f: (0, f)),
                  pl.BlockSpec((tf, d), lambda b, i, f: (f, 0)),
                  rows(),
                  pl.BlockSpec((1, 1, d), lambda b, i, f: (b, 0, 0)),
                  pl.BlockSpec((1, d), lambda b, i, f: (0, 0))],
        out_specs=rows(),
        out_shape=jax.ShapeDtypeStruct((bsz, s, d), F32),
        scratch_shapes=[pltpu.VMEM((tm, d), F32)],
        compiler_params=pltpu.CompilerParams(
            dimension_semantics=("arbitrary", "arbitrary", "arbitrary"),
            vmem_limit_bytes=_vmem_limit(vmem)),
        name="ffn",
    )(h2, wg, wu, wd, x1, gate, final_g.reshape(1, d))


def kernel(x, c, w_ada, b_ada, norm1_g, w_in, lam_q1, lam_k1, lam_q2, lam_k2, diff_subln_g,
           ret_gn_w, ret_gn_b, w_out, norm2_g, w_ffn_gate, w_ffn_up, w_ffn_down, final_g):
    bsz, s, d = x.shape
    depth = w_ada.shape[0]
    assert depth == 1, "the final RMSNorm is fused into the FFN kernel of the only layer"
    n_heads = d // (2 * HEAD_DIM)
    diff_qk_w = n_heads * 2 * HEAD_DIM
    ret_col0 = 2 * diff_qk_w + d
    slopes = jnp.asarray([2.0 ** (-8.0 * (h + 1) / n_heads) for h in range(n_heads)], F32)
    log_g = jnp.asarray([math.log(1.0 - 2.0 ** (-5 - h)) for h in range(n_heads)], F32)

    for l in range(depth):
        lam_init = 0.8 - 0.6 * math.exp(-0.3 * l)
        mod = _adaln(c, w_ada[l], b_ada[l])
        shift1, scale1, gate1, shift2, scale2, gate2 = (
            m.reshape(bsz, 1, d) for m in jnp.split(mod, 6, axis=-1))
        proj = _inproj(x, norm1_g[l], scale1, shift1, w_in[l].astype(BF16))
        ya = _diff_attention(proj, slopes, lam_q1[l:l + 1], lam_k1[l:l + 1], lam_q2[l:l + 1],
                             lam_k2[l:l + 1], diff_subln_g[l], n_heads=n_heads, d_model=d,
                             lam_init=lam_init)
        merged = _retention_merge(proj, ya, log_g, ret_gn_w[l], ret_gn_b[l],
                                  n_heads=n_heads, d_model=d, col0=ret_col0)
        x1, h2 = _outproj(merged, w_out[l].astype(BF16), x, gate1, norm2_g[l], scale2, shift2)
        x = _ffn(h2, w_ffn_gate[l].astype(BF16), w_ffn_up[l].astype(BF16),
                 w_ffn_down[l].astype(BF16), x1, gate2, final_g)
    return x
```

```python
import functools
import math

import jax
import jax.numpy as jnp
from jax import lax
from jax.experimental import pallas as pl
from jax.experimental.pallas import tpu as pltpu

F32 = jnp.float32
BF16 = jnp.bfloat16

HEAD_DIM = 128
NORM_EPS = 1e-6
LOG2E = math.log2(math.e)

V7X_VMEM_BYTES = 64 * 1024 * 1024
V7X_SUBLANES = 8

ADA_TN = 1024
INPROJ_TM = 1024
INPROJ_TN = 1024
ATTN_BLOCK = 512
ATTN_STRIP = 256
RET_BLOCK = 512
OUTPROJ_TM = 512
FFN_TM = 512
FFN_TF = 512


def _vmem_limit(nbytes):
    return int(min(nbytes + 16 * 1024 * 1024, V7X_VMEM_BYTES - 4 * 1024 * 1024))


def _sigmoid(x):
    return 1.0 / (1.0 + jnp.exp(-x))


def _adaln_kernel(c_ref, w_ref, b_ref, o_ref):
    c = c_ref[...]
    sc = c * _sigmoid(c)
    o_ref[...] = jnp.dot(sc, w_ref[...], preferred_element_type=F32,
                         precision=lax.Precision.HIGHEST) + b_ref[...]


def _adaln(c, w, b):
    bsz, d = c.shape
    n = w.shape[1]
    rows = -(-bsz // V7X_SUBLANES) * V7X_SUBLANES
    c_pad = jnp.pad(c, ((0, rows - bsz), (0, 0)))
    out = pl.pallas_call(
        _adaln_kernel,
        grid=(n // ADA_TN,),
        in_specs=[pl.BlockSpec((rows, d), lambda j: (0, 0)),
                  pl.BlockSpec((d, ADA_TN), lambda j: (0, j)),
                  pl.BlockSpec((1, ADA_TN), lambda j: (0, j))],
        out_specs=pl.BlockSpec((rows, ADA_TN), lambda j: (0, j)),
        out_shape=jax.ShapeDtypeStruct((rows, n), F32),
        compiler_params=pltpu.CompilerParams(
            dimension_semantics=("arbitrary",),
            vmem_limit_bytes=_vmem_limit(2 * d * ADA_TN * 4)),
        name="adaln_mod",
    )(c_pad, w, b.reshape(1, n))
    return out[:bsz]


def _modulated_rmsnorm(x, g, scale, shift):
    ms = jnp.mean(x * x, axis=-1, keepdims=True)
    y = x * lax.rsqrt(ms + NORM_EPS) * g
    return y * (1.0 + scale) + shift


def _inproj_kernel(x_ref, g_ref, scale_ref, shift_ref, w_ref, o_ref, h_ref):
    @pl.when(pl.program_id(2) == 0)
    def _():
        h = _modulated_rmsnorm(x_ref[0], g_ref[...], scale_ref[0], shift_ref[0])
        h_ref[...] = h.astype(BF16)

    o_ref[0] = jnp.dot(h_ref[...], w_ref[...], preferred_element_type=F32).astype(o_ref.dtype)


def _inproj(x, g, scale, shift, w_bf16):
    bsz, s, d = x.shape
    n = w_bf16.shape[1]
    tm, tn = INPROJ_TM, INPROJ_TN
    vmem = 2 * tm * d * 4 + 2 * d * tn * 2 + 2 * tm * tn * 2 + tm * d * 2
    return pl.pallas_call(
        _inproj_kernel,
        grid=(bsz, s // tm, n // tn),
        in_specs=[pl.BlockSpec((1, tm, d), lambda b, i, j: (b, i, 0)),
                  pl.BlockSpec((1, d), lambda b, i, j: (0, 0)),
                  pl.BlockSpec((1, 1, d), lambda b, i, j: (b, 0, 0)),
                  pl.BlockSpec((1, 1, d), lambda b, i, j: (b, 0, 0)),
                  pl.BlockSpec((d, tn), lambda b, i, j: (0, j))],
        out_specs=pl.BlockSpec((1, tm, tn), lambda b, i, j: (b, i, j)),
        out_shape=jax.ShapeDtypeStruct((bsz, s, n), BF16),
        scratch_shapes=[pltpu.VMEM((tm, d), BF16)],
        compiler_params=pltpu.CompilerParams(
            dimension_semantics=("arbitrary", "arbitrary", "arbitrary"),
            vmem_limit_bytes=_vmem_limit(vmem)),
        name="inproj",
    )(x, g.reshape(1, d), scale, shift, w_bf16)


def _diff_attn_kernel(slopes_ref, q1_ref, q2_ref, k1_ref, k2_ref, v_ref,
                      lq1_ref, lk1_ref, lq2_ref, lk2_ref, g_ref, o_ref,
                      bias_ref, vt_ref, acc1_ref, acc2_ref, p1_ref, p2_ref, t2_ref, *, lam_init, blk):
    h = pl.program_id(1)
    i = pl.program_id(2)
    slope_l2 = slopes_ref[h] * LOG2E
    n_kv = v_ref.shape[1] // blk

    @pl.when(i == 0)
    def _():
        key = lax.broadcasted_iota(jnp.int32, (blk, blk), 0)
        qry = lax.broadcasted_iota(jnp.int32, (blk, blk), 1)
        bias = (key - qry).astype(F32) * slope_l2
        bias_ref[0] = bias
        bias_ref[1] = jnp.where(qry >= key, bias, -jnp.inf)

        def xpose(c, _):
            start = pl.multiple_of(c * blk, blk)
            vt_ref[:, pl.ds(start, blk)] = v_ref[0, pl.ds(start, blk), :].astype(F32).T.astype(BF16)
            return 0
        lax.fori_loop(0, n_kv, xpose, 0)

    qscale = (HEAD_DIM ** -0.5) * LOG2E
    q1t = (q1_ref[0].astype(F32) * qscale).T.astype(BF16)
    q2t = (q2_ref[0].astype(F32) * qscale).T.astype(BF16)

    acc1_ref[...] = jnp.zeros_like(acc1_ref)
    acc2_ref[...] = jnp.zeros_like(acc2_ref)
    strips = [slice(c, c + ATTN_STRIP) for c in range(0, blk, ATTN_STRIP)]
    cat = lambda xs: jnp.concatenate(xs, axis=1)

    def scores(qt, k_ref, start):
        k = k_ref[0, pl.ds(start, blk), :]
        return [jnp.dot(k, qt[:, cs], preferred_element_type=F32) for cs in strips]

    def tile_offset(j):
        return slope_l2 * ((i - j) * blk).astype(F32)

    def softmax(ts, mxs, off, m_old, l_old, p_ref):
        m_out, l_out, a_out = [], [], []
        for cs, t, mx in zip(strips, ts, mxs):
            m_new = jnp.maximum(m_old[:, cs], mx)
            p = jnp.exp2(t() - (m_new + off))
            alpha = jnp.exp2(m_old[:, cs] - m_new)
            l_out.append(alpha * l_old[:, cs] + jnp.sum(p, axis=0, keepdims=True))
            m_out.append(m_new)
            a_out.append(alpha)
            p_ref[:, cs] = p.astype(BF16)
        return cat(m_out), cat(l_out), cat(a_out)

    def softmax_now(ss, sel, off, m_old, l_old, p_ref):
        ts = [s + bias_ref[sel, :, cs] for cs, s in zip(strips, ss)]
        mxs = [jnp.max(t, axis=0, keepdims=True) - off for t in ts]
        return softmax([lambda t=t: t for t in ts], mxs, off, m_old, l_old, p_ref)

    def park_scores(ss, sel, off):
        mxs = []
        for cs, s in zip(strips, ss):
            t = s + bias_ref[sel, :, cs]
            t2_ref[:, cs] = t
            mxs.append(jnp.max(t, axis=0, keepdims=True) - off)
        return cat(mxs)

    def softmax_parked(mx, off, m_old, l_old, p_ref):
        ts = [lambda cs=cs: t2_ref[:, cs] for cs in strips]
        return softmax(ts, [mx[:, cs] for cs in strips], off, m_old, l_old, p_ref)

    def pv(vt, acc_ref, p_ref, alpha):
        acc_ref[...] = alpha * acc_ref[...] + jnp.dot(vt, p_ref[...], preferred_element_type=F32)

    neg = jnp.full((1, blk), -jnp.inf, F32)
    zero = jnp.zeros((1, blk), F32)

    sel0 = (i == 0).astype(jnp.int32)
    off0 = tile_offset(0)
    s1 = scores(q1t, k1_ref, 0)
    s2 = scores(q2t, k2_ref, 0)
    m1, l1, a1 = softmax_now(s1, sel0, off0, neg, zero, p1_ref)
    mx2 = park_scores(s2, sel0, off0)

    def step(j, carry):
        m1, l1, a1, m2, l2, mx2 = carry
        sel = (j == i).astype(jnp.int32)
        start = pl.multiple_of(j * blk, blk)
        vt_prev = vt_ref[:, pl.ds(pl.multiple_of((j - 1) * blk, blk), blk)]
        off = tile_offset(j)
        s1 = scores(q1t, k1_ref, start)
        pv(vt_prev, acc1_ref, p1_ref, a1)
        m2, l2, a2 = softmax_parked(mx2, tile_offset(j - 1), m2, l2, p2_ref)
        s2 = scores(q2t, k2_ref, start)
        m1, l1, a1 = softmax_now(s1, sel, off, m1, l1, p1_ref)
        pv(vt_prev, acc2_ref, p2_ref, a2)
        mx2 = park_scores(s2, sel, off)
        return m1, l1, a1, m2, l2, mx2

    m1, l1, a1, m2, l2, mx2 = lax.fori_loop(1, i + 1, step, (m1, l1, a1, neg, zero, mx2))

    vt_last = vt_ref[:, pl.ds(pl.multiple_of(i * blk, blk), blk)]
    pv(vt_last, acc1_ref, p1_ref, a1)
    m2, l2, a2 = softmax_parked(mx2, tile_offset(i), m2, l2, p2_ref)
    pv(vt_last, acc2_ref, p2_ref, a2)

    lam = (jnp.exp(jnp.sum(lq1_ref[...] * lk1_ref[...], axis=-1, keepdims=True))
           - jnp.exp(jnp.sum(lq2_ref[...] * lk2_ref[...], axis=-1, keepdims=True))
           + lam_init)
    y = (acc1_ref[...] / l1 - lam * (acc2_ref[...] / l2)).T
    yn = y * lax.rsqrt(jnp.mean(y * y, axis=-1, keepdims=True) + NORM_EPS)
    o_ref[0] = (yn * g_ref[...] * (1.0 - lam_init)).astype(o_ref.dtype)


def _diff_attention(proj, slopes, lq1, lk1, lq2, lk2, subln_g, *, n_heads, d_model, lam_init):
    bsz, s, _ = proj.shape
    blk = ATTN_BLOCK
    dv = d_model // n_heads
    k_col0 = (n_heads * 2 * HEAD_DIM) // HEAD_DIM
    v_col0 = (2 * n_heads * 2 * HEAD_DIM) // dv
    qspec = lambda m: pl.BlockSpec((1, blk, HEAD_DIM), lambda b, h, i, sl: (b, i, 2 * h + m))
    kspec = lambda m: pl.BlockSpec((1, s, HEAD_DIM), lambda b, h, i, sl: (b, 0, k_col0 + 2 * h + m))
    vec = pl.BlockSpec((1, HEAD_DIM), lambda b, h, i, sl: (0, 0))
    vmem = (2 * 2 * s * HEAD_DIM * 2 + 3 * s * dv * 2 + 2 * blk * blk * 4 + 2 * blk * dv * 4
            + 6 * blk * blk * 4)
    kernel = functools.partial(_diff_attn_kernel, lam_init=lam_init, blk=blk)
    return pl.pallas_call(
        kernel,
        grid_spec=pltpu.PrefetchScalarGridSpec(
            num_scalar_prefetch=1,
            grid=(bsz, n_heads, s // blk),
            in_specs=[qspec(0), qspec(1), kspec(0), kspec(1),
                      pl.BlockSpec((1, s, dv), lambda b, h, i, sl: (b, 0, v_col0 + h)),
                      vec, vec, vec, vec,
                      pl.BlockSpec((1, dv), lambda b, h, i, sl: (0, h))],
            out_specs=pl.BlockSpec((1, blk, dv), lambda b, h, i, sl: (b, i, h)),
            scratch_shapes=[pltpu.VMEM((2, blk, blk), F32),
                            pltpu.VMEM((dv, s), BF16),
                            pltpu.VMEM((dv, blk), F32), pltpu.VMEM((dv, blk), F32),
                            pltpu.VMEM((blk, blk), BF16), pltpu.VMEM((blk, blk), BF16),
                            pltpu.VMEM((blk, blk), F32)]),
        out_shape=jax.ShapeDtypeStruct((bsz, s, d_model), BF16),
        compiler_params=pltpu.CompilerParams(
            dimension_semantics=("arbitrary", "arbitrary", "arbitrary"),
            vmem_limit_bytes=_vmem_limit(vmem)),
        name="diff_attention",
    )(slopes, proj, proj, proj, proj, proj, lq1, lk1, lq2, lk2, subln_g.reshape(1, d_model))


def _retention_merge_kernel(logg_ref, q_ref, k_ref, v_ref, rg_ref, ga_ref, gb_ref, ya_ref,
                            gnw_ref, gnb_ref, o_ref, decay_ref, state_ref, *, blk):
    h = pl.program_id(1)
    n = pl.program_id(2)
    log_g = logg_ref[h]
    kscale = HEAD_DIM ** -0.5

    @pl.when(n == 0)
    def _():
        row = lax.broadcasted_iota(jnp.int32, (blk, blk), 0)
        col = lax.broadcasted_iota(jnp.int32, (blk, blk), 1)
        rel = (row - col).astype(F32)
        decay_ref[...] = jnp.where(rel >= 0, jnp.exp(log_g * jnp.maximum(rel, 0.0)) * kscale, 0.0)
        state_ref[...] = jnp.zeros_like(state_ref)

    q = q_ref[0]
    k = k_ref[0]
    v = v_ref[0]
    idx = lax.broadcasted_iota(jnp.int32, (blk, 1), 0).astype(F32)
    xi = jnp.exp(log_g * (idx + 1.0))
    zeta = jnp.exp(log_g * (blk - 1.0 - idx)) * kscale

    s = lax.dot_general(q, k, (((1,), (1,)), ((), ())), preferred_element_type=F32)
    intra = jnp.dot((s * decay_ref[...]).astype(BF16), v, preferred_element_type=F32)
    state = state_ref[...]
    cross = jnp.dot((q.astype(F32) * xi).astype(BF16), state.astype(BF16), preferred_element_type=F32)
    y = intra + cross

    kz = (k.astype(F32) * zeta).astype(BF16)
    kv = lax.dot_general(kz, v, (((0,), (0,)), ((), ())), preferred_element_type=F32)
    state_ref[...] = jnp.exp(log_g * blk) * state + kv

    mu = jnp.mean(y, axis=-1, keepdims=True)
    yc = y - mu
    var = jnp.mean(yc * yc, axis=-1, keepdims=True)
    gn = yc * lax.rsqrt(var + NORM_EPS) * gnw_ref[...] + gnb_ref[...]
    rg = rg_ref[0].astype(F32)
    yb = rg * _sigmoid(rg) * gn
    merged = (_sigmoid(ga_ref[0].astype(F32)) * ya_ref[0].astype(F32)
              + _sigmoid(gb_ref[0].astype(F32)) * yb)
    o_ref[0] = merged.astype(o_ref.dtype)


def _retention_merge(proj, ya, log_g, gn_w, gn_b, *, n_heads, d_model, col0):
    bsz, s, _ = proj.shape
    blk = RET_BLOCK
    dv = d_model // n_heads
    q0 = col0 // HEAD_DIM
    k0 = q0 + n_heads
    v0 = (col0 + 2 * n_heads * HEAD_DIM) // dv
    rg0 = v0 + n_heads
    ga0 = rg0 + n_heads
    gb0 = ga0 + n_heads
    wide = lambda c0: pl.BlockSpec((1, blk, dv), lambda b, h, n, lg: (b, n, c0 + h))
    narrow = lambda c0: pl.BlockSpec((1, blk, HEAD_DIM), lambda b, h, n, lg: (b, n, c0 + h))
    chan = pl.BlockSpec((1, dv), lambda b, h, n, lg: (0, h))
    vmem = 2 * (2 * blk * HEAD_DIM * 2 + 6 * blk * dv * 2) + blk * blk * 4 + HEAD_DIM * dv * 4 \
        + 4 * blk * blk * 4
    kernel = functools.partial(_retention_merge_kernel, blk=blk)
    return pl.pallas_call(
        kernel,
        grid_spec=pltpu.PrefetchScalarGridSpec(
            num_scalar_prefetch=1,
            grid=(bsz, n_heads, s // blk),
            in_specs=[narrow(q0), narrow(k0), wide(v0), wide(rg0), wide(ga0), wide(gb0),
                      wide(0), chan, chan],
            out_specs=wide(0),
            scratch_shapes=[pltpu.VMEM((blk, blk), F32), pltpu.VMEM((HEAD_DIM, dv), F32)]),
        out_shape=jax.ShapeDtypeStruct((bsz, s, d_model), BF16),
        compiler_params=pltpu.CompilerParams(
            dimension_semantics=("arbitrary", "arbitrary", "arbitrary"),
            vmem_limit_bytes=_vmem_limit(vmem)),
        name="retention_merge",
    )(log_g, proj, proj, proj, proj, proj, proj, ya, gn_w.reshape(1, d_model), gn_b.reshape(1, d_model))


def _outproj_kernel(m_ref, w_ref, x_ref, gate_ref, g_ref, scale_ref, shift_ref, x1_ref, h2_ref):
    y = jnp.dot(m_ref[0], w_ref[...], preferred_element_type=F32)
    x1 = x_ref[0] + gate_ref[0] * y
    x1_ref[0] = x1
    h2_ref[0] = _modulated_rmsnorm(x1, g_ref[...], scale_ref[0], shift_ref[0]).astype(h2_ref.dtype)


def _outproj(merged, w_bf16, x, gate, g, scale, shift):
    bsz, s, d = x.shape
    tm = OUTPROJ_TM
    rows = pl.BlockSpec((1, tm, d), lambda b, i: (b, i, 0))
    mod = pl.BlockSpec((1, 1, d), lambda b, i: (b, 0, 0))
    vmem = 2 * (tm * d * 2 + d * d * 2 + tm * d * 4 + tm * d * 4 + tm * d * 2)
    return pl.pallas_call(
        _outproj_kernel,
        grid=(bsz, s // tm),
        in_specs=[rows, pl.BlockSpec((d, d), lambda b, i: (0, 0)), rows, mod,
                  pl.BlockSpec((1, d), lambda b, i: (0, 0)), mod, mod],
        out_specs=[rows, rows],
        out_shape=[jax.ShapeDtypeStruct((bsz, s, d), F32), jax.ShapeDtypeStruct((bsz, s, d), BF16)],
        compiler_params=pltpu.CompilerParams(
            dimension_semantics=("arbitrary", "arbitrary"),
            vmem_limit_bytes=_vmem_limit(vmem)),
        name="outproj",
    )(merged, w_bf16, x, gate, g.reshape(1, d), scale, shift)


def _ffn_kernel(h_ref, wg_ref, wu_ref, wd_ref, x1_ref, gate_ref, fg_ref, o_ref, acc_ref):
    f = pl.program_id(2)

    @pl.when(f == 0)
    def _():
        acc_ref[...] = jnp.zeros_like(acc_ref)

    h = h_ref[0]
    g = jnp.dot(h, wg_ref[...], preferred_element_type=F32)
    u = jnp.dot(h, wu_ref[...], preferred_element_type=F32)
    a = (g * _sigmoid(g) * u).astype(BF16)
    acc_ref[...] += jnp.dot(a, wd_ref[...], preferred_element_type=F32)

    @pl.when(f == pl.num_programs(2) - 1)
    def _():
        x2 = x1_ref[0] + gate_ref[0] * acc_ref[...]
        ms = jnp.mean(x2 * x2, axis=-1, keepdims=True)
        o_ref[0] = x2 * lax.rsqrt(ms + NORM_EPS) * fg_ref[...]


def _ffn(h2, wg, wu, wd, x1, gate, final_g):
    bsz, s, d = x1.shape
    dff = wg.shape[1]
    tm, tf = FFN_TM, FFN_TF
    rows = lambda: pl.BlockSpec((1, tm, d), lambda b, i, f: (b, i, 0))
    vmem = 2 * (tm * d * 2 + 2 * d * tf * 2 + tf * d * 2 + tm * d * 4 + tm * d * 4) + tm * d * 4
    return pl.pallas_call(
        _ffn_kernel,
        grid=(bsz, s // tm, dff // tf),
        in_specs=[rows(),
                  pl.BlockSpec((d, tf), lambda b, i, f: (0, f)),
                  pl.BlockSpec((d, tf), lambda b, i, f: (0, f)),
                  pl.BlockSpec((tf, d), lambda b, i, f: (f, 0)),
                  rows(),
                  pl.BlockSpec((1, 1, d), lambda b, i, f: (b, 0, 0)),
                  pl.BlockSpec((1, d), lambda b, i, f: (0, 0))],
        out_specs=rows(),
        out_shape=jax.ShapeDtypeStruct((bsz, s, d), F32),
        scratch_shapes=[pltpu.VMEM((tm, d), F32)],
        compiler_params=pltpu.CompilerParams(
            dimension_semantics=("arbitrary", "arbitrary", "arbitrary"),
            vmem_limit_bytes=_vmem_limit(vmem)),
        name="ffn",
    )(h2, wg, wu, wd, x1, gate, final_g.reshape(1, d))


def kernel(x, c, w_ada, b_ada, norm1_g, w_in, lam_q1, lam_k1, lam_q2, lam_k2, diff_subln_g,
           ret_gn_w, ret_gn_b, w_out, norm2_g, w_ffn_gate, w_ffn_up, w_ffn_down, final_g):
    bsz, s, d = x.shape
    depth = w_ada.shape[0]
    assert depth == 1, "the final RMSNorm is fused into the FFN kernel of the only layer"
    n_heads = d // (2 * HEAD_DIM)
    diff_qk_w = n_heads * 2 * HEAD_DIM
    ret_col0 = 2 * diff_qk_w + d
    slopes = jnp.asarray([2.0 ** (-8.0 * (h + 1) / n_heads) for h in range(n_heads)], F32)
    log_g = jnp.asarray([math.log(1.0 - 2.0 ** (-5 - h)) for h in range(n_heads)], F32)

    for l in range(depth):
        lam_init = 0.8 - 0.6 * math.exp(-0.3 * l)
        mod = _adaln(c, w_ada[l], b_ada[l])
        shift1, scale1, gate1, shift2, scale2, gate2 = (
            m.reshape(bsz, 1, d) for m in jnp.split(mod, 6, axis=-1))
        proj = _inproj(x, norm1_g[l], scale1, shift1, w_in[l].astype(BF16))
        ya = _diff_attention(proj, slopes, lam_q1[l:l + 1], lam_k1[l:l + 1], lam_q2[l:l + 1],
                             lam_k2[l:l + 1], diff_subln_g[l], n_heads=n_heads, d_model=d,
                             lam_init=lam_init)
        merged = _retention_merge(proj, ya, log_g, ret_gn_w[l], ret_gn_b[l],
                                  n_heads=n_heads, d_model=d, col0=ret_col0)
        x1, h2 = _outproj(merged, w_out[l].astype(BF16), x, gate1, norm2_g[l], scale2, shift2)
        x = _ffn(h2, w_ffn_gate[l].astype(BF16), w_ffn_up[l].astype(BF16),
                 w_ffn_down[l].astype(BF16), x1, gate2, final_g)
    return x
```

```python
import functools
import math

import jax
import jax.numpy as jnp
from jax import lax
from jax.experimental import pallas as pl
from jax.experimental.pallas import tpu as pltpu

F32 = jnp.float32
BF16 = jnp.bfloat16

HEAD_DIM = 128
NORM_EPS = 1e-6
LOG2E = math.log2(math.e)

V7X_VMEM_BYTES = 64 * 1024 * 1024
V7X_SUBLANES = 8

ADA_TN = 1024
INPROJ_TM = 1024
INPROJ_TN = 1024
ATTN_BLOCK = 512
ATTN_STRIP = 256
RET_BLOCK = 512
RET_HEADS = 2
OUTPROJ_TM = 512
FFN_TM = 512
FFN_TF = 512


def _vmem_limit(nbytes):
    return int(min(nbytes + 16 * 1024 * 1024, V7X_VMEM_BYTES - 4 * 1024 * 1024))


def _sigmoid(x):
    return 1.0 / (1.0 + jnp.exp(-x))


def _adaln_kernel(c_ref, w_ref, b_ref, o_ref):
    c = c_ref[...]
    sc = c * _sigmoid(c)
    o_ref[...] = jnp.dot(sc, w_ref[...], preferred_element_type=F32,
                         precision=lax.Precision.HIGHEST) + b_ref[...]


def _adaln(c, w, b):
    bsz, d = c.shape
    n = w.shape[1]
    rows = -(-bsz // V7X_SUBLANES) * V7X_SUBLANES
    c_pad = jnp.pad(c, ((0, rows - bsz), (0, 0)))
    out = pl.pallas_call(
        _adaln_kernel,
        grid=(n // ADA_TN,),
        in_specs=[pl.BlockSpec((rows, d), lambda j: (0, 0)),
                  pl.BlockSpec((d, ADA_TN), lambda j: (0, j)),
                  pl.BlockSpec((1, ADA_TN), lambda j: (0, j))],
        out_specs=pl.BlockSpec((rows, ADA_TN), lambda j: (0, j)),
        out_shape=jax.ShapeDtypeStruct((rows, n), F32),
        compiler_params=pltpu.CompilerParams(
            dimension_semantics=("arbitrary",),
            vmem_limit_bytes=_vmem_limit(2 * d * ADA_TN * 4)),
        name="adaln_mod",
    )(c_pad, w, b.reshape(1, n))
    return out[:bsz]


def _modulated_rmsnorm(x, g, scale, shift):
    ms = jnp.mean(x * x, axis=-1, keepdims=True)
    y = x * lax.rsqrt(ms + NORM_EPS) * g
    return y * (1.0 + scale) + shift


def _inproj_kernel(x_ref, g_ref, scale_ref, shift_ref, w_ref, o_ref, h_ref):
    @pl.when(pl.program_id(2) == 0)
    def _():
        h = _modulated_rmsnorm(x_ref[0], g_ref[...], scale_ref[0], shift_ref[0])
        h_ref[...] = h.astype(BF16)

    w = w_ref[...].astype(BF16)
    o_ref[0] = jnp.dot(h_ref[...], w, preferred_element_type=F32).astype(o_ref.dtype)


def _inproj(x, g, scale, shift, w):
    bsz, s, d = x.shape
    n = w.shape[1]
    tm, tn = INPROJ_TM, INPROJ_TN
    vmem = 2 * tm * d * 4 + 2 * d * tn * 4 + d * tn * 2 + 2 * tm * tn * 2 + tm * d * 2
    return pl.pallas_call(
        _inproj_kernel,
        grid=(bsz, s // tm, n // tn),
        in_specs=[pl.BlockSpec((1, tm, d), lambda b, i, j: (b, i, 0)),
                  pl.BlockSpec((1, d), lambda b, i, j: (0, 0)),
                  pl.BlockSpec((1, 1, d), lambda b, i, j: (b, 0, 0)),
                  pl.BlockSpec((1, 1, d), lambda b, i, j: (b, 0, 0)),
                  pl.BlockSpec((d, tn), lambda b, i, j: (0, j))],
        out_specs=pl.BlockSpec((1, tm, tn), lambda b, i, j: (b, i, j)),
        out_shape=jax.ShapeDtypeStruct((bsz, s, n), BF16),
        scratch_shapes=[pltpu.VMEM((tm, d), BF16)],
        compiler_params=pltpu.CompilerParams(
            dimension_semantics=("arbitrary", "arbitrary", "arbitrary"),
            vmem_limit_bytes=_vmem_limit(vmem)),
        name="inproj",
    )(x, g.reshape(1, d), scale, shift, w)


def _diff_attn_kernel(slopes_ref, q1_ref, q2_ref, k1_ref, k2_ref, v_ref,
                      lq1_ref, lk1_ref, lq2_ref, lk2_ref, g_ref, o_ref,
                      bias_ref, vt_ref, acc1_ref, acc2_ref, p1_ref, p2_ref, t2_ref, *, lam_init, blk):
    h = pl.program_id(1)
    i = pl.program_id(2)
    slope_l2 = slopes_ref[h] * LOG2E
    n_kv = v_ref.shape[1] // blk

    @pl.when(i == 0)
    def _():
        key = lax.broadcasted_iota(jnp.int32, (blk, blk), 0)
        qry = lax.broadcasted_iota(jnp.int32, (blk, blk), 1)
        bias = (key - qry).astype(F32) * slope_l2
        bias_ref[0] = bias
        bias_ref[1] = jnp.where(qry >= key, bias, -jnp.inf)

        def xpose(c, _):
            start = pl.multiple_of(c * blk, blk)
            vt_ref[:, pl.ds(start, blk)] = v_ref[0, pl.ds(start, blk), :].astype(F32).T.astype(BF16)
            return 0
        lax.fori_loop(0, n_kv, xpose, 0)

    qscale = (HEAD_DIM ** -0.5) * LOG2E
    q1t = (q1_ref[0].astype(F32) * qscale).T.astype(BF16)
    q2t = (q2_ref[0].astype(F32) * qscale).T.astype(BF16)

    acc1_ref[...] = jnp.zeros_like(acc1_ref)
    acc2_ref[...] = jnp.zeros_like(acc2_ref)
    strips = [slice(c, c + ATTN_STRIP) for c in range(0, blk, ATTN_STRIP)]
    cat = lambda xs: jnp.concatenate(xs, axis=1)

    def scores(qt, k_ref, start):
        k = k_ref[0, pl.ds(start, blk), :]
        return [jnp.dot(k, qt[:, cs], preferred_element_type=F32) for cs in strips]

    def tile_offset(j):
        return slope_l2 * ((i - j) * blk).astype(F32)

    def softmax(ts, mxs, off, m_old, l_old, p_ref):
        m_out, l_out, a_out = [], [], []
        for cs, t, mx in zip(strips, ts, mxs):
            m_new = jnp.maximum(m_old[:, cs], mx)
            p = jnp.exp2(t() - (m_new + off))
            alpha = jnp.exp2(m_old[:, cs] - m_new)
            l_out.append(alpha * l_old[:, cs] + jnp.sum(p, axis=0, keepdims=True))
            m_out.append(m_new)
            a_out.append(alpha)
            p_ref[:, cs] = p.astype(BF16)
        return cat(m_out), cat(l_out), cat(a_out)

    def softmax_now(ss, sel, off, m_old, l_old, p_ref):
        ts = [s + bias_ref[sel, :, cs] for cs, s in zip(strips, ss)]
        mxs = [jnp.max(t, axis=0, keepdims=True) - off for t in ts]
        return softmax([lambda t=t: t for t in ts], mxs, off, m_old, l_old, p_ref)

    def park_scores(ss, sel, off):
        mxs = []
        for cs, s in zip(strips, ss):
            t = s + bias_ref[sel, :, cs]
            t2_ref[:, cs] = t
            mxs.append(jnp.max(t, axis=0, keepdims=True) - off)
        return cat(mxs)

    def softmax_parked(mx, off, m_old, l_old, p_ref):
        ts = [lambda cs=cs: t2_ref[:, cs] for cs in strips]
        return softmax(ts, [mx[:, cs] for cs in strips], off, m_old, l_old, p_ref)

    def pv(vt, acc_ref, p_ref, alpha):
        acc_ref[...] = alpha * acc_ref[...] + jnp.dot(vt, p_ref[...], preferred_element_type=F32)

    neg = jnp.full((1, blk), -jnp.inf, F32)
    zero = jnp.zeros((1, blk), F32)

    sel0 = (i == 0).astype(jnp.int32)
    off0 = tile_offset(0)
    s1 = scores(q1t, k1_ref, 0)
    s2 = scores(q2t, k2_ref, 0)
    m1, l1, a1 = softmax_now(s1, sel0, off0, neg, zero, p1_ref)
    mx2 = park_scores(s2, sel0, off0)

    def step(j, carry):
        m1, l1, a1, m2, l2, mx2 = carry
        sel = (j == i).astype(jnp.int32)
        start = pl.multiple_of(j * blk, blk)
        vt_prev = vt_ref[:, pl.ds(pl.multiple_of((j - 1) * blk, blk), blk)]
        off = tile_offset(j)
        s1 = scores(q1t, k1_ref, start)
        pv(vt_prev, acc1_ref, p1_ref, a1)
        m2, l2, a2 = softmax_parked(mx2, tile_offset(j - 1), m2, l2, p2_ref)
        s2 = scores(q2t, k2_ref, start)
        m1, l1, a1 = softmax_now(s1, sel, off, m1, l1, p1_ref)
        pv(vt_prev, acc2_ref, p2_ref, a2)
        mx2 = park_scores(s2, sel, off)
        return m1, l1, a1, m2, l2, mx2

    m1, l1, a1, m2, l2, mx2 = lax.fori_loop(1, i + 1, step, (m1, l1, a1, neg, zero, mx2))

    vt_last = vt_ref[:, pl.ds(pl.multiple_of(i * blk, blk), blk)]
    pv(vt_last, acc1_ref, p1_ref, a1)
    m2, l2, a2 = softmax_parked(mx2, tile_offset(i), m2, l2, p2_ref)
    pv(vt_last, acc2_ref, p2_ref, a2)

    lam = (jnp.exp(jnp.sum(lq1_ref[...] * lk1_ref[...], axis=-1, keepdims=True))
           - jnp.exp(jnp.sum(lq2_ref[...] * lk2_ref[...], axis=-1, keepdims=True))
           + lam_init)
    y = (acc1_ref[...] / l1 - lam * (acc2_ref[...] / l2)).T
    yn = y * lax.rsqrt(jnp.mean(y * y, axis=-1, keepdims=True) + NORM_EPS)
    o_ref[0] = (yn * g_ref[...] * (1.0 - lam_init)).astype(o_ref.dtype)


def _diff_attention(proj, slopes, lq1, lk1, lq2, lk2, subln_g, *, n_heads, d_model, lam_init):
    bsz, s, _ = proj.shape
    blk = ATTN_BLOCK
    dv = d_model // n_heads
    k_col0 = (n_heads * 2 * HEAD_DIM) // HEAD_DIM
    v_col0 = (2 * n_heads * 2 * HEAD_DIM) // dv
    qspec = lambda m: pl.BlockSpec((1, blk, HEAD_DIM), lambda b, h, i, sl: (b, i, 2 * h + m))
    kspec = lambda m: pl.BlockSpec((1, s, HEAD_DIM), lambda b, h, i, sl: (b, 0, k_col0 + 2 * h + m))
    vec = pl.BlockSpec((1, HEAD_DIM), lambda b, h, i, sl: (0, 0))
    vmem = (2 * 2 * s * HEAD_DIM * 2 + 3 * s * dv * 2 + 2 * blk * blk * 4 + 2 * blk * dv * 4
            + 6 * blk * blk * 4)
    kernel = functools.partial(_diff_attn_kernel, lam_init=lam_init, blk=blk)
    return pl.pallas_call(
        kernel,
        grid_spec=pltpu.PrefetchScalarGridSpec(
            num_scalar_prefetch=1,
            grid=(bsz, n_heads, s // blk),
            in_specs=[qspec(0), qspec(1), kspec(0), kspec(1),
                      pl.BlockSpec((1, s, dv), lambda b, h, i, sl: (b, 0, v_col0 + h)),
                      vec, vec, vec, vec,
                      pl.BlockSpec((1, dv), lambda b, h, i, sl: (0, h))],
            out_specs=pl.BlockSpec((1, blk, dv), lambda b, h, i, sl: (b, i, h)),
            scratch_shapes=[pltpu.VMEM((2, blk, blk), F32),
                            pltpu.VMEM((dv, s), BF16),
                            pltpu.VMEM((dv, blk), F32), pltpu.VMEM((dv, blk), F32),
                            pltpu.VMEM((blk, blk), BF16), pltpu.VMEM((blk, blk), BF16),
                            pltpu.VMEM((blk, blk), F32)]),
        out_shape=jax.ShapeDtypeStruct((bsz, s, d_model), BF16),
        compiler_params=pltpu.CompilerParams(
            dimension_semantics=("arbitrary", "arbitrary", "arbitrary"),
            vmem_limit_bytes=_vmem_limit(vmem)),
        name="diff_attention",
    )(slopes, proj, proj, proj, proj, proj, lq1, lk1, lq2, lk2, subln_g.reshape(1, d_model))


def _retention_merge_kernel(logg_ref, q_ref, k_ref, v_ref, rg_ref, ga_ref, gb_ref, ya_ref,
                            gnw_ref, gnb_ref, o_ref, decay_ref, state_ref, xi_ref, zeta_ref, *, blk, dv):
    hp = pl.program_id(1)
    n = pl.program_id(2)
    kscale = HEAD_DIM ** -0.5
    heads = range(RET_HEADS)
    log_gs = [logg_ref[hp * RET_HEADS + hh] for hh in heads]

    @pl.when(n == 0)
    def _():
        row = lax.broadcasted_iota(jnp.int32, (blk, blk), 0)
        col = lax.broadcasted_iota(jnp.int32, (blk, blk), 1)
        rel = (row - col).astype(F32)
        idx = lax.broadcasted_iota(jnp.int32, (blk, 1), 0).astype(F32)
        for hh in heads:
            decay_ref[hh] = jnp.where(rel >= 0, jnp.exp(log_gs[hh] * jnp.maximum(rel, 0.0)) * kscale, 0.0)
            xi_ref[hh] = jnp.exp(log_gs[hh] * (idx + 1.0))
            zeta_ref[hh] = jnp.exp(log_gs[hh] * (blk - 1.0 - idx)) * kscale
        state_ref[...] = jnp.zeros_like(state_ref)

    def mix(hh):
        log_g = log_gs[hh]
        qk = slice(hh * HEAD_DIM, (hh + 1) * HEAD_DIM)
        vv = slice(hh * dv, (hh + 1) * dv)
        q = q_ref[0, :, qk]
        k = k_ref[0, :, qk]
        v = v_ref[0, :, vv]
        s = lax.dot_general(q, k, (((1,), (1,)), ((), ())), preferred_element_type=F32)
        intra = jnp.dot((s * decay_ref[hh]).astype(BF16), v, preferred_element_type=F32)
        state = state_ref[hh]
        cross = jnp.dot((q.astype(F32) * xi_ref[hh]).astype(BF16), state.astype(BF16),
                        preferred_element_type=F32)
        kz = (k.astype(F32) * zeta_ref[hh]).astype(BF16)
        kv = lax.dot_general(kz, v, (((0,), (0,)), ((), ())), preferred_element_type=F32)
        state_ref[hh] = jnp.exp(log_g * blk) * state + kv
        return intra + cross

    def finish(hh, y):
        vv = slice(hh * dv, (hh + 1) * dv)
        mu = jnp.mean(y, axis=-1, keepdims=True)
        yc = y - mu
        var = jnp.mean(yc * yc, axis=-1, keepdims=True)
        gn = yc * lax.rsqrt(var + NORM_EPS) * gnw_ref[:, vv] + gnb_ref[:, vv]
        rg = rg_ref[0, :, vv].astype(F32)
        yb = rg * _sigmoid(rg) * gn
        merged = (_sigmoid(ga_ref[0, :, vv].astype(F32)) * ya_ref[0, :, vv].astype(F32)
                  + _sigmoid(gb_ref[0, :, vv].astype(F32)) * yb)
        o_ref[0, :, vv] = merged.astype(o_ref.dtype)

    ys = [mix(hh) for hh in heads]
    for hh in heads:
        finish(hh, ys[hh])


def _retention_merge(proj, ya, log_g, gn_w, gn_b, *, n_heads, d_model, col0):
    bsz, s, _ = proj.shape
    blk = RET_BLOCK
    dv = d_model // n_heads
    nw = RET_HEADS * HEAD_DIM
    ww = RET_HEADS * dv
    groups = n_heads // RET_HEADS
    q0 = col0 // nw
    k0 = q0 + groups
    v0 = (col0 + 2 * n_heads * HEAD_DIM) // ww
    rg0 = v0 + groups
    ga0 = rg0 + groups
    gb0 = ga0 + groups
    wide = lambda c0: pl.BlockSpec((1, blk, ww), lambda b, h, n, lg: (b, n, c0 + h))
    narrow = lambda c0: pl.BlockSpec((1, blk, nw), lambda b, h, n, lg: (b, n, c0 + h))
    chan = pl.BlockSpec((1, ww), lambda b, h, n, lg: (0, h))
    vmem = 2 * (2 * blk * nw * 2 + 6 * blk * ww * 2) + RET_HEADS * (blk * blk * 4 + HEAD_DIM * dv * 4) \
        + 4 * RET_HEADS * blk * blk * 4
    kernel = functools.partial(_retention_merge_kernel, blk=blk, dv=dv)
    return pl.pallas_call(
        kernel,
        grid_spec=pltpu.PrefetchScalarGridSpec(
            num_scalar_prefetch=1,
            grid=(bsz, groups, s // blk),
            in_specs=[narrow(q0), narrow(k0), wide(v0), wide(rg0), wide(ga0), wide(gb0),
                      wide(0), chan, chan],
            out_specs=wide(0),
            scratch_shapes=[pltpu.VMEM((RET_HEADS, blk, blk), F32),
                            pltpu.VMEM((RET_HEADS, HEAD_DIM, dv), F32),
                            pltpu.VMEM((RET_HEADS, blk, 1), F32), pltpu.VMEM((RET_HEADS, blk, 1), F32)]),
        out_shape=jax.ShapeDtypeStruct((bsz, s, d_model), BF16),
        compiler_params=pltpu.CompilerParams(
            dimension_semantics=("arbitrary", "arbitrary", "arbitrary"),
            vmem_limit_bytes=_vmem_limit(vmem)),
        name="retention_merge",
    )(log_g, proj, proj, proj, proj, proj, proj, ya, gn_w.reshape(1, d_model), gn_b.reshape(1, d_model))


def _outproj_kernel(m_ref, w_ref, x_ref, gate_ref, g_ref, scale_ref, shift_ref, x1_ref, h2_ref):
    y = jnp.dot(m_ref[0], w_ref[...], preferred_element_type=F32)
    x1 = x_ref[0] + gate_ref[0] * y
    x1_ref[0] = x1
    h2_ref[0] = _modulated_rmsnorm(x1, g_ref[...], scale_ref[0], shift_ref[0]).astype(h2_ref.dtype)


def _outproj(merged, w_bf16, x, gate, g, scale, shift):
    bsz, s, d = x.shape
    tm = OUTPROJ_TM
    rows = pl.BlockSpec((1, tm, d), lambda b, i: (b, i, 0))
    mod = pl.BlockSpec((1, 1, d), lambda b, i: (b, 0, 0))
    vmem = 2 * (tm * d * 2 + d * d * 2 + tm * d * 4 + tm * d * 4 + tm * d * 2)
    return pl.pallas_call(
        _outproj_kernel,
        grid=(bsz, s // tm),
        in_specs=[rows, pl.BlockSpec((d, d), lambda b, i: (0, 0)), rows, mod,
                  pl.BlockSpec((1, d), lambda b, i: (0, 0)), mod, mod],
        out_specs=[rows, rows],
        out_shape=[jax.ShapeDtypeStruct((bsz, s, d), F32), jax.ShapeDtypeStruct((bsz, s, d), BF16)],
        compiler_params=pltpu.CompilerParams(
            dimension_semantics=("arbitrary", "arbitrary"),
            vmem_limit_bytes=_vmem_limit(vmem)),
        name="outproj",
    )(merged, w_bf16, x, gate, g.reshape(1, d), scale, shift)


def _ffn_kernel(h_ref, wg_ref, wu_ref, wd_ref, x1_ref, gate_ref, fg_ref, o_ref, acc_ref):
    f = pl.program_id(2)

    @pl.when(f == 0)
    def _():
        acc_ref[...] = jnp.zeros_like(acc_ref)

    h = h_ref[0]
    g = jnp.dot(h, wg_ref[...], preferred_element_type=F32)
    u = jnp.dot(h, wu_ref[...], preferred_element_type=F32)
    a = (g * _sigmoid(g) * u).astype(BF16)
    acc_ref[...] += jnp.dot(a, wd_ref[...], preferred_element_type=F32)

    @pl.when(f == pl.num_programs(2) - 1)
    def _():
        x2 = x1_ref[0] + gate_ref[0] * acc_ref[...]
        ms = jnp.mean(x2 * x2, axis=-1, keepdims=True)
        o_ref[0] = x2 * lax.rsqrt(ms + NORM_EPS) * fg_ref[...]


def _ffn(h2, wg, wu, wd, x1, gate, final_g):
    bsz, s, d = x1.shape
    dff = wg.shape[1]
    tm, tf = FFN_TM, FFN_TF
    rows = lambda: pl.BlockSpec((1, tm, d), lambda b, i, f: (b, i, 0))
    vmem = 2 * (tm * d * 2 + 2 * d * tf * 2 + tf * d * 2 + tm * d * 4 + tm * d * 4) + tm * d * 4
    return pl.pallas_call(
        _ffn_kernel,
        grid=(bsz, s // tm, dff // tf),
        in_specs=[rows(),
                  pl.BlockSpec((d, tf), lambda b, i, f: (0, f)),
                  pl.BlockSpec((d, tf), lambda b, i, f: (0, f)),
                  pl.BlockSpec((tf, d), lambda b, i, f: (f, 0)),
                  rows(),
                  pl.BlockSpec((1, 1, d), lambda b, i, f: (b, 0, 0)),
                  pl.BlockSpec((1, d), lambda b, i, f: (0, 0))],
        out_specs=rows(),
        out_shape=jax.ShapeDtypeStruct((bsz, s, d), F32),
        scratch_shapes=[pltpu.VMEM((tm, d), F32)],
        compiler_params=pltpu.CompilerParams(
            dimension_semantics=("arbitrary", "arbitrary", "arbitrary"),
            vmem_limit_bytes=_vmem_limit(vmem)),
        name="ffn",
    )(h2, wg, wu, wd, x1, gate, final_g.reshape(1, d))


def kernel(x, c, w_ada, b_ada, norm1_g, w_in, lam_q1, lam_k1, lam_q2, lam_k2, diff_subln_g,
           ret_gn_w, ret_gn_b, w_out, norm2_g, w_ffn_gate, w_ffn_up, w_ffn_down, final_g):
    bsz, s, d = x.shape
    depth = w_ada.shape[0]
    assert depth == 1, "the final RMSNorm is fused into the FFN kernel of the only layer"
    n_heads = d // (2 * HEAD_DIM)
    diff_qk_w = n_heads * 2 * HEAD_DIM
    ret_col0 = 2 * diff_qk_w + d
    slopes = jnp.asarray([2.0 ** (-8.0 * (h + 1) / n_heads) for h in range(n_heads)], F32)
    log_g = jnp.asarray([math.log(1.0 - 2.0 ** (-5 - h)) for h in range(n_heads)], F32)

    for l in range(depth):
        lam_init = 0.8 - 0.6 * math.exp(-0.3 * l)
        mod = _adaln(c, w_ada[l], b_ada[l])
        shift1, scale1, gate1, shift2, scale2, gate2 = (
            m.reshape(bsz, 1, d) for m in jnp.split(mod, 6, axis=-1))
        proj = _inproj(x, norm1_g[l], scale1, shift1, w_in[l])
        ya = _diff_attention(proj, slopes, lam_q1[l:l + 1], lam_k1[l:l + 1], lam_q2[l:l + 1],
                             lam_k2[l:l + 1], diff_subln_g[l], n_heads=n_heads, d_model=d,
                             lam_init=lam_init)
        merged = _retention_merge(proj, ya, log_g, ret_gn_w[l], ret_gn_b[l],
                                  n_heads=n_heads, d_model=d, col0=ret_col0)
        x1, h2 = _outproj(merged, w_out[l].astype(BF16), x, gate1, norm2_g[l], scale2, shift2)
        x = _ffn(h2, w_ffn_gate[l].astype(BF16), w_ffn_up[l].astype(BF16),
                 w_ffn_down[l].astype(BF16), x1, gate2, final_g)
    return x
```

```python
import functools
import math

import jax
import jax.numpy as jnp
from jax import lax
from jax.experimental import pallas as pl
from jax.experimental.pallas import tpu as pltpu

F32 = jnp.float32
BF16 = jnp.bfloat16

HEAD_DIM = 128
NORM_EPS = 1e-6
LOG2E = math.log2(math.e)

V7X_VMEM_BYTES = 64 * 1024 * 1024
V7X_SUBLANES = 8

ADA_TN = 1024
INPROJ_TM = 1024
INPROJ_TN = 1024
ATTN_BQ = 512
ATTN_BK = 512
ATTN_STRIP = 256
RET_BLOCK = 512
RET_HEADS = 4
OUTPROJ_TM = 512
FFN_TM = 512
FFN_TF = 512


def _vmem_limit(nbytes):
    return int(min(nbytes + 16 * 1024 * 1024, V7X_VMEM_BYTES - 4 * 1024 * 1024))


def _sigmoid(x):
    return 1.0 / (1.0 + jnp.exp(-x))


def _adaln_kernel(c_ref, w_ref, b_ref, o_ref):
    c = c_ref[...]
    sc = c * _sigmoid(c)
    o_ref[...] = jnp.dot(sc, w_ref[...], preferred_element_type=F32,
                         precision=lax.Precision.HIGHEST) + b_ref[...]


def _adaln(c, w, b):
    bsz, d = c.shape
    n = w.shape[1]
    rows = -(-bsz // V7X_SUBLANES) * V7X_SUBLANES
    c_pad = jnp.pad(c, ((0, rows - bsz), (0, 0)))
    out = pl.pallas_call(
        _adaln_kernel,
        grid=(n // ADA_TN,),
        in_specs=[pl.BlockSpec((rows, d), lambda j: (0, 0)),
                  pl.BlockSpec((d, ADA_TN), lambda j: (0, j)),
                  pl.BlockSpec((1, ADA_TN), lambda j: (0, j))],
        out_specs=pl.BlockSpec((rows, ADA_TN), lambda j: (0, j)),
        out_shape=jax.ShapeDtypeStruct((rows, n), F32),
        compiler_params=pltpu.CompilerParams(
            dimension_semantics=("arbitrary",),
            vmem_limit_bytes=_vmem_limit(2 * d * ADA_TN * 4)),
        name="adaln_mod",
    )(c_pad, w, b.reshape(1, n))
    return out[:bsz]


def _modulated_rmsnorm(x, g, scale, shift):
    ms = jnp.mean(x * x, axis=-1, keepdims=True)
    y = x * lax.rsqrt(ms + NORM_EPS) * g
    return y * (1.0 + scale) + shift


def _inproj_kernel(x_ref, g_ref, scale_ref, shift_ref, w_ref, o_ref, h_ref):
    @pl.when(pl.program_id(2) == 0)
    def _():
        h = _modulated_rmsnorm(x_ref[0], g_ref[...], scale_ref[0], shift_ref[0])
        h_ref[...] = h.astype(BF16)

    w = w_ref[...].astype(BF16)
    o_ref[0] = jnp.dot(h_ref[...], w, preferred_element_type=F32).astype(o_ref.dtype)


def _inproj(x, g, scale, shift, w):
    bsz, s, d = x.shape
    n = w.shape[1]
    tm, tn = INPROJ_TM, INPROJ_TN
    vmem = 2 * tm * d * 4 + 2 * d * tn * 4 + d * tn * 2 + 2 * tm * tn * 2 + tm * d * 2
    return pl.pallas_call(
        _inproj_kernel,
        grid=(bsz, s // tm, n // tn),
        in_specs=[pl.BlockSpec((1, tm, d), lambda b, i, j: (b, i, 0)),
                  pl.BlockSpec((1, d), lambda b, i, j: (0, 0)),
                  pl.BlockSpec((1, 1, d), lambda b, i, j: (b, 0, 0)),
                  pl.BlockSpec((1, 1, d), lambda b, i, j: (b, 0, 0)),
                  pl.BlockSpec((d, tn), lambda b, i, j: (0, j))],
        out_specs=pl.BlockSpec((1, tm, tn), lambda b, i, j: (b, i, j)),
        out_shape=jax.ShapeDtypeStruct((bsz, s, n), BF16),
        scratch_shapes=[pltpu.VMEM((tm, d), BF16)],
        compiler_params=pltpu.CompilerParams(
            dimension_semantics=("arbitrary", "arbitrary", "arbitrary"),
            vmem_limit_bytes=_vmem_limit(vmem)),
        name="inproj",
    )(x, g.reshape(1, d), scale, shift, w)


def _diff_attn_kernel(slopes_ref, q1_ref, q2_ref, k1_ref, k2_ref, v_ref,
                      lq1_ref, lk1_ref, lq2_ref, lk2_ref, g_ref, o_ref,
                      bias_ref, vt_ref, acc1_ref, acc2_ref, p1_ref, p2_ref, t2_ref, *, lam_init, bq, bk):
    h = pl.program_id(1)
    i = pl.program_id(2)
    slope_l2 = slopes_ref[h] * LOG2E
    n_kv = v_ref.shape[1] // bk
    ratio = bq // bk

    @pl.when(i == 0)
    def _():
        key = lax.broadcasted_iota(jnp.int32, (bk, bq), 0)
        qry = lax.broadcasted_iota(jnp.int32, (bk, bq), 1)
        bias = (key - qry).astype(F32) * slope_l2
        bias_ref[0] = bias
        for d in range(ratio):
            bias_ref[1 + d] = jnp.where(qry >= key + d * bk, bias, -jnp.inf)

        def xpose(c, _):
            start = pl.multiple_of(c * bk, bk)
            vt_ref[:, pl.ds(start, bk)] = v_ref[0, pl.ds(start, bk), :].astype(F32).T.astype(BF16)
            return 0
        lax.fori_loop(0, n_kv, xpose, 0)

    qscale = (HEAD_DIM ** -0.5) * LOG2E
    q1t = (q1_ref[0].astype(F32) * qscale).T.astype(BF16)
    q2t = (q2_ref[0].astype(F32) * qscale).T.astype(BF16)

    acc1_ref[...] = jnp.zeros_like(acc1_ref)
    acc2_ref[...] = jnp.zeros_like(acc2_ref)
    strips = [slice(c, c + ATTN_STRIP) for c in range(0, bq, ATTN_STRIP)]
    cat = lambda xs: jnp.concatenate(xs, axis=1)

    def scores(qt, k_ref, start):
        k = k_ref[0, pl.ds(start, bk), :]
        return [jnp.dot(k, qt[:, cs], preferred_element_type=F32) for cs in strips]

    def tile_offset(j):
        return slope_l2 * (i * bq - j * bk).astype(F32)

    def bias_variant(j):
        d = j - ratio * i
        return jnp.where(d >= 0, d + 1, 0)

    def softmax(ts, mxs, off, m_old, l_old, p_ref):
        m_out, l_out, a_out = [], [], []
        for cs, t, mx in zip(strips, ts, mxs):
            m_new = jnp.maximum(m_old[:, cs], mx)
            p = jnp.exp2(t() - (m_new + off))
            alpha = jnp.exp2(m_old[:, cs] - m_new)
            l_out.append(alpha * l_old[:, cs] + jnp.sum(p, axis=0, keepdims=True))
            m_out.append(m_new)
            a_out.append(alpha)
            p_ref[:, cs] = p.astype(BF16)
        return cat(m_out), cat(l_out), cat(a_out)

    def softmax_now(ss, sel, off, m_old, l_old, p_ref):
        ts = [s + bias_ref[sel, :, cs] for cs, s in zip(strips, ss)]
        mxs = [jnp.max(t, axis=0, keepdims=True) - off for t in ts]
        return softmax([lambda t=t: t for t in ts], mxs, off, m_old, l_old, p_ref)

    def park_scores(ss, sel, off):
        mxs = []
        for cs, s in zip(strips, ss):
            t = s + bias_ref[sel, :, cs]
            t2_ref[:, cs] = t
            mxs.append(jnp.max(t, axis=0, keepdims=True) - off)
        return cat(mxs)

    def softmax_parked(mx, off, m_old, l_old, p_ref):
        ts = [lambda cs=cs: t2_ref[:, cs] for cs in strips]
        return softmax(ts, [mx[:, cs] for cs in strips], off, m_old, l_old, p_ref)

    def pv(vt, acc_ref, p_ref, alpha):
        acc_ref[...] = alpha * acc_ref[...] + jnp.dot(vt, p_ref[...], preferred_element_type=F32)

    neg = jnp.full((1, bq), -jnp.inf, F32)
    zero = jnp.zeros((1, bq), F32)
    last = ratio * i + ratio - 1

    sel0 = bias_variant(0)
    off0 = tile_offset(0)
    s1 = scores(q1t, k1_ref, 0)
    s2 = scores(q2t, k2_ref, 0)
    m1, l1, a1 = softmax_now(s1, sel0, off0, neg, zero, p1_ref)
    mx2 = park_scores(s2, sel0, off0)

    def step(j, carry):
        m1, l1, a1, m2, l2, mx2 = carry
        sel = bias_variant(j)
        start = pl.multiple_of(j * bk, bk)
        vt_prev = vt_ref[:, pl.ds(pl.multiple_of((j - 1) * bk, bk), bk)]
        off = tile_offset(j)
        s1 = scores(q1t, k1_ref, start)
        pv(vt_prev, acc1_ref, p1_ref, a1)
        m2, l2, a2 = softmax_parked(mx2, tile_offset(j - 1), m2, l2, p2_ref)
        s2 = scores(q2t, k2_ref, start)
        m1, l1, a1 = softmax_now(s1, sel, off, m1, l1, p1_ref)
        pv(vt_prev, acc2_ref, p2_ref, a2)
        mx2 = park_scores(s2, sel, off)
        return m1, l1, a1, m2, l2, mx2

    m1, l1, a1, m2, l2, mx2 = lax.fori_loop(1, last + 1, step, (m1, l1, a1, neg, zero, mx2))

    vt_last = vt_ref[:, pl.ds(pl.multiple_of(last * bk, bk), bk)]
    pv(vt_last, acc1_ref, p1_ref, a1)
    m2, l2, a2 = softmax_parked(mx2, tile_offset(last), m2, l2, p2_ref)
    pv(vt_last, acc2_ref, p2_ref, a2)

    lam = (jnp.exp(jnp.sum(lq1_ref[...] * lk1_ref[...], axis=-1, keepdims=True))
           - jnp.exp(jnp.sum(lq2_ref[...] * lk2_ref[...], axis=-1, keepdims=True))
           + lam_init)
    yt = acc1_ref[...] * (1.0 / l1) - acc2_ref[...] * (lam / l2)
    ynt = yt * lax.rsqrt(jnp.mean(yt * yt, axis=0, keepdims=True) + NORM_EPS)
    o_ref[0] = (ynt.T * (g_ref[...] * (1.0 - lam_init))).astype(o_ref.dtype)


def _diff_attention(proj, slopes, lq1, lk1, lq2, lk2, subln_g, *, n_heads, d_model, lam_init):
    bsz, s, _ = proj.shape
    bq, bk = ATTN_BQ, ATTN_BK
    dv = d_model // n_heads
    k_col0 = (n_heads * 2 * HEAD_DIM) // HEAD_DIM
    v_col0 = (2 * n_heads * 2 * HEAD_DIM) // dv
    qspec = lambda m: pl.BlockSpec((1, bq, HEAD_DIM), lambda b, h, i, sl: (b, i, 2 * h + m))
    kspec = lambda m: pl.BlockSpec((1, s, HEAD_DIM), lambda b, h, i, sl: (b, 0, k_col0 + 2 * h + m))
    vec = pl.BlockSpec((1, HEAD_DIM), lambda b, h, i, sl: (0, 0))
    n_bias = 1 + bq // bk
    vmem = (2 * 2 * s * HEAD_DIM * 2 + 3 * s * dv * 2 + (n_bias + 1) * bk * bq * 4 + 2 * bq * dv * 4
            + 2 * bk * bq * 2 + 6 * bk * bq * 4)
    kernel = functools.partial(_diff_attn_kernel, lam_init=lam_init, bq=bq, bk=bk)
    return pl.pallas_call(
        kernel,
        grid_spec=pltpu.PrefetchScalarGridSpec(
            num_scalar_prefetch=1,
            grid=(bsz, n_heads, s // bq),
            in_specs=[qspec(0), qspec(1), kspec(0), kspec(1),
                      pl.BlockSpec((1, s, dv), lambda b, h, i, sl: (b, 0, v_col0 + h)),
                      vec, vec, vec, vec,
                      pl.BlockSpec((1, dv), lambda b, h, i, sl: (0, h))],
            out_specs=pl.BlockSpec((1, bq, dv), lambda b, h, i, sl: (b, i, h)),
            scratch_shapes=[pltpu.VMEM((n_bias, bk, bq), F32),
                            pltpu.VMEM((dv, s), BF16),
                            pltpu.VMEM((dv, bq), F32), pltpu.VMEM((dv, bq), F32),
                            pltpu.VMEM((bk, bq), BF16), pltpu.VMEM((bk, bq), BF16),
                            pltpu.VMEM((bk, bq), F32)]),
        out_shape=jax.ShapeDtypeStruct((bsz, s, d_model), BF16),
        compiler_params=pltpu.CompilerParams(
            dimension_semantics=("arbitrary", "arbitrary", "arbitrary"),
            vmem_limit_bytes=_vmem_limit(vmem)),
        name="diff_attention",
    )(slopes, proj, proj, proj, proj, proj, lq1, lk1, lq2, lk2, subln_g.reshape(1, d_model))


def _retention_merge_kernel(logg_ref, q_ref, k_ref, v_ref, rg_ref, ga_ref, gb_ref, ya_ref,
                            gnw_ref, gnb_ref, o_ref, decay_ref, state_ref, xi_ref, zeta_ref, *, blk, dv):
    hp = pl.program_id(1)
    n = pl.program_id(2)
    kscale = HEAD_DIM ** -0.5
    heads = range(RET_HEADS)
    log_gs = [logg_ref[hp * RET_HEADS + hh] for hh in heads]

    @pl.when(n == 0)
    def _():
        row = lax.broadcasted_iota(jnp.int32, (blk, blk), 0)
        col = lax.broadcasted_iota(jnp.int32, (blk, blk), 1)
        rel = (row - col).astype(F32)
        idx = lax.broadcasted_iota(jnp.int32, (blk, 1), 0).astype(F32)
        for hh in heads:
            decay_ref[hh] = jnp.where(rel >= 0, jnp.exp(log_gs[hh] * jnp.maximum(rel, 0.0)) * kscale, 0.0)
            xi_ref[hh] = jnp.exp(log_gs[hh] * (idx + 1.0))
            zeta_ref[hh] = jnp.exp(log_gs[hh] * (blk - 1.0 - idx)) * kscale
        state_ref[...] = jnp.zeros_like(state_ref)

    def mix(hh):
        log_g = log_gs[hh]
        qk = slice(hh * HEAD_DIM, (hh + 1) * HEAD_DIM)
        vv = slice(hh * dv, (hh + 1) * dv)
        q = q_ref[0, :, qk]
        k = k_ref[0, :, qk]
        v = v_ref[0, :, vv]
        s = lax.dot_general(q, k, (((1,), (1,)), ((), ())), preferred_element_type=F32)
        intra = jnp.dot((s * decay_ref[hh]).astype(BF16), v, preferred_element_type=F32)
        state = state_ref[hh]
        cross = jnp.dot((q.astype(F32) * xi_ref[hh]).astype(BF16), state.astype(BF16),
                        preferred_element_type=F32)
        kz = (k.astype(F32) * zeta_ref[hh]).astype(BF16)
        kv = lax.dot_general(kz, v, (((0,), (0,)), ((), ())), preferred_element_type=F32)
        state_ref[hh] = jnp.exp(log_g * blk) * state + kv
        return intra + cross

    def finish(hh, y):
        vv = slice(hh * dv, (hh + 1) * dv)
        mu = jnp.mean(y, axis=-1, keepdims=True)
        yc = y - mu
        var = jnp.mean(yc * yc, axis=-1, keepdims=True)
        gn = yc * lax.rsqrt(var + NORM_EPS) * gnw_ref[:, vv] + gnb_ref[:, vv]
        rg = rg_ref[0, :, vv].astype(F32)
        yb = rg * _sigmoid(rg) * gn
        merged = (_sigmoid(ga_ref[0, :, vv].astype(F32)) * ya_ref[0, :, vv].astype(F32)
                  + _sigmoid(gb_ref[0, :, vv].astype(F32)) * yb)
        o_ref[0, :, vv] = merged.astype(o_ref.dtype)

    ys = [mix(hh) for hh in heads]
    for hh in heads:
        finish(hh, ys[hh])


def _retention_merge(proj, ya, log_g, gn_w, gn_b, *, n_heads, d_model, col0):
    bsz, s, _ = proj.shape
    blk = RET_BLOCK
    dv = d_model // n_heads
    nw = RET_HEADS * HEAD_DIM
    ww = RET_HEADS * dv
    groups = n_heads // RET_HEADS
    q0 = col0 // nw
    k0 = q0 + groups
    v0 = (col0 + 2 * n_heads * HEAD_DIM) // ww
    rg0 = v0 + groups
    ga0 = rg0 + groups
    gb0 = ga0 + groups
    wide = lambda c0: pl.BlockSpec((1, blk, ww), lambda b, h, n, lg: (b, n, c0 + h))
    narrow = lambda c0: pl.BlockSpec((1, blk, nw), lambda b, h, n, lg: (b, n, c0 + h))
    chan = pl.BlockSpec((1, ww), lambda b, h, n, lg: (0, h))
    vmem = 2 * (2 * blk * nw * 2 + 6 * blk * ww * 2) + RET_HEADS * (blk * blk * 4 + HEAD_DIM * dv * 4) \
        + 4 * RET_HEADS * blk * blk * 4
    kernel = functools.partial(_retention_merge_kernel, blk=blk, dv=dv)
    return pl.pallas_call(
        kernel,
        grid_spec=pltpu.PrefetchScalarGridSpec(
            num_scalar_prefetch=1,
            grid=(bsz, groups, s // blk),
            in_specs=[narrow(q0), narrow(k0), wide(v0), wide(rg0), wide(ga0), wide(gb0),
                      wide(0), chan, chan],
            out_specs=wide(0),
            scratch_shapes=[pltpu.VMEM((RET_HEADS, blk, blk), F32),
                            pltpu.VMEM((RET_HEADS, HEAD_DIM, dv), F32),
                            pltpu.VMEM((RET_HEADS, blk, 1), F32), pltpu.VMEM((RET_HEADS, blk, 1), F32)]),
        out_shape=jax.ShapeDtypeStruct((bsz, s, d_model), BF16),
        compiler_params=pltpu.CompilerParams(
            dimension_semantics=("arbitrary", "arbitrary", "arbitrary"),
            vmem_limit_bytes=_vmem_limit(vmem)),
        name="retention_merge",
    )(log_g, proj, proj, proj, proj, proj, proj, ya, gn_w.reshape(1, d_model), gn_b.reshape(1, d_model))


def _outproj_kernel(m_ref, w_ref, x_ref, gate_ref, g_ref, scale_ref, shift_ref, x1_ref, h2_ref):
    half = m_ref.shape[1] // 2
    rows = [slice(0, half), slice(half, 2 * half)]
    ys = [jnp.dot(m_ref[0, r, :], w_ref[...], preferred_element_type=F32) for r in rows]
    for r, y in zip(rows, ys):
        x1 = x_ref[0, r, :] + gate_ref[0] * y
        x1_ref[0, r, :] = x1
        h2_ref[0, r, :] = _modulated_rmsnorm(x1, g_ref[...], scale_ref[0],
                                             shift_ref[0]).astype(h2_ref.dtype)


def _outproj(merged, w_bf16, x, gate, g, scale, shift):
    bsz, s, d = x.shape
    tm = OUTPROJ_TM
    rows = pl.BlockSpec((1, tm, d), lambda b, i: (b, i, 0))
    mod = pl.BlockSpec((1, 1, d), lambda b, i: (b, 0, 0))
    vmem = 2 * (tm * d * 2 + d * d * 2 + tm * d * 4 + tm * d * 4 + tm * d * 2)
    return pl.pallas_call(
        _outproj_kernel,
        grid=(bsz, s // tm),
        in_specs=[rows, pl.BlockSpec((d, d), lambda b, i: (0, 0)), rows, mod,
                  pl.BlockSpec((1, d), lambda b, i: (0, 0)), mod, mod],
        out_specs=[rows, rows],
        out_shape=[jax.ShapeDtypeStruct((bsz, s, d), F32), jax.ShapeDtypeStruct((bsz, s, d), BF16)],
        compiler_params=pltpu.CompilerParams(
            dimension_semantics=("arbitrary", "arbitrary"),
            vmem_limit_bytes=_vmem_limit(vmem)),
        name="outproj",
    )(merged, w_bf16, x, gate, g.reshape(1, d), scale, shift)


def _ffn_kernel(h_ref, wg_ref, wu_ref, wd_ref, x1_ref, gate_ref, fg_ref, o_ref, acc_ref):
    f = pl.program_id(2)

    @pl.when(f == 0)
    def _():
        acc_ref[...] = jnp.zeros_like(acc_ref)

    h = h_ref[0]
    g = jnp.dot(h, wg_ref[...], preferred_element_type=F32)
    u = jnp.dot(h, wu_ref[...], preferred_element_type=F32)
    a = (g * _sigmoid(g) * u).astype(BF16)
    acc_ref[...] += jnp.dot(a, wd_ref[...], preferred_element_type=F32)

    @pl.when(f == pl.num_programs(2) - 1)
    def _():
        x2 = x1_ref[0] + gate_ref[0] * acc_ref[...]
        ms = jnp.mean(x2 * x2, axis=-1, keepdims=True)
        o_ref[0] = x2 * lax.rsqrt(ms + NORM_EPS) * fg_ref[...]


def _ffn(h2, wg, wu, wd, x1, gate, final_g):
    bsz, s, d = x1.shape
    dff = wg.shape[1]
    tm, tf = FFN_TM, FFN_TF
    rows = lambda: pl.BlockSpec((1, tm, d), lambda b, i, f: (b, i, 0))
    vmem = 2 * (tm * d * 2 + 2 * d * tf * 2 + tf * d * 2 + tm * d * 4 + tm * d * 4) + tm * d * 4
    return pl.pallas_call(
        _ffn_kernel,
        grid=(bsz, s // tm, dff // tf),
        in_specs=[rows(),
                  pl.BlockSpec((d, tf), lambda b, i, f: (0, f)),
                  pl.BlockSpec((d, tf), lambda b, i, f: (0, f)),
                  pl.BlockSpec((tf, d), lambda b, i, f: (f, 0)),
                  rows(),
                  pl.BlockSpec((1, 1, d), lambda b, i, f: (b, 0, 0)),
                  pl.BlockSpec((1, d), lambda b, i, f: (0, 0))],
        out_specs=rows(),
        out_shape=jax.ShapeDtypeStruct((bsz, s, d), F32),
        scratch_shapes=[pltpu.VMEM((tm, d), F32)],
        compiler_params=pltpu.CompilerParams(
            dimension_semantics=("arbitrary", "arbitrary", "arbitrary"),
            vmem_limit_bytes=_vmem_limit(vmem)),
        name="ffn",
    )(h2, wg, wu, wd, x1, gate, final_g.reshape(1, d))


def kernel(x, c, w_ada, b_ada, norm1_g, w_in, lam_q1, lam_k1, lam_q2, lam_k2, diff_subln_g,
           ret_gn_w, ret_gn_b, w_out, norm2_g, w_ffn_gate, w_ffn_up, w_ffn_down, final_g):
    bsz, s, d = x.shape
    depth = w_ada.shape[0]
    assert depth == 1, "the final RMSNorm is fused into the FFN kernel of the only layer"
    n_heads = d // (2 * HEAD_DIM)
    diff_qk_w = n_heads * 2 * HEAD_DIM
    ret_col0 = 2 * diff_qk_w + d
    slopes = jnp.asarray([2.0 ** (-8.0 * (h + 1) / n_heads) for h in range(n_heads)], F32)
    log_g = jnp.asarray([math.log(1.0 - 2.0 ** (-5 - h)) for h in range(n_heads)], F32)

    for l in range(depth):
        lam_init = 0.8 - 0.6 * math.exp(-0.3 * l)
        mod = _adaln(c, w_ada[l], b_ada[l])
        shift1, scale1, gate1, shift2, scale2, gate2 = (
            m.reshape(bsz, 1, d) for m in jnp.split(mod, 6, axis=-1))
        proj = _inproj(x, norm1_g[l], scale1, shift1, w_in[l])
        ya = _diff_attention(proj, slopes, lam_q1[l:l + 1], lam_k1[l:l + 1], lam_q2[l:l + 1],
                             lam_k2[l:l + 1], diff_subln_g[l], n_heads=n_heads, d_model=d,
                             lam_init=lam_init)
        merged = _retention_merge(proj, ya, log_g, ret_gn_w[l], ret_gn_b[l],
                                  n_heads=n_heads, d_model=d, col0=ret_col0)
        x1, h2 = _outproj(merged, w_out[l].astype(BF16), x, gate1, norm2_g[l], scale2, shift2)
        x = _ffn(h2, w_ffn_gate[l].astype(BF16), w_ffn_up[l].astype(BF16),
                 w_ffn_down[l].astype(BF16), x1, gate2, final_g)
    return x
```

```python
import functools
import math

import jax
import jax.numpy as jnp
from jax import lax
from jax.experimental import pallas as pl
from jax.experimental.pallas import tpu as pltpu

F32 = jnp.float32
BF16 = jnp.bfloat16

HEAD_DIM = 128
NORM_EPS = 1e-6
LOG2E = math.log2(math.e)

V7X_VMEM_BYTES = 64 * 1024 * 1024
V7X_SUBLANES = 8

ADA_TN = 1024
INPROJ_TM = 1024
INPROJ_TN = 1024
ATTN_BQ = 512
ATTN_BK = 512
RET_BLOCK = 512
RET_HEADS = 4
OUTPROJ_TM = 512
FFN_TM = 512
FFN_TF = 512


def _vmem_limit(nbytes):
    return int(min(nbytes + 16 * 1024 * 1024, V7X_VMEM_BYTES - 4 * 1024 * 1024))


def _sigmoid(x):
    return 1.0 / (1.0 + jnp.exp(-x))


def _adaln_kernel(c_ref, w_ref, b_ref, o_ref):
    c = c_ref[...]
    sc = c * _sigmoid(c)
    o_ref[...] = jnp.dot(sc, w_ref[...], preferred_element_type=F32,
                         precision=lax.Precision.HIGHEST) + b_ref[...]


def _adaln(c, w, b):
    bsz, d = c.shape
    n = w.shape[1]
    rows = -(-bsz // V7X_SUBLANES) * V7X_SUBLANES
    c_pad = jnp.pad(c, ((0, rows - bsz), (0, 0)))
    out = pl.pallas_call(
        _adaln_kernel,
        grid=(n // ADA_TN,),
        in_specs=[pl.BlockSpec((rows, d), lambda j: (0, 0)),
                  pl.BlockSpec((d, ADA_TN), lambda j: (0, j)),
                  pl.BlockSpec((1, ADA_TN), lambda j: (0, j))],
        out_specs=pl.BlockSpec((rows, ADA_TN), lambda j: (0, j)),
        out_shape=jax.ShapeDtypeStruct((rows, n), F32),
        compiler_params=pltpu.CompilerParams(
            dimension_semantics=("arbitrary",),
            vmem_limit_bytes=_vmem_limit(2 * d * ADA_TN * 4)),
        name="adaln_mod",
    )(c_pad, w, b.reshape(1, n))
    return out[:bsz]


def _modulated_rmsnorm(x, g, scale, shift):
    ms = jnp.mean(x * x, axis=-1, keepdims=True)
    y = x * lax.rsqrt(ms + NORM_EPS) * g
    return y * (1.0 + scale) + shift


def _inproj_kernel(x_ref, g_ref, scale_ref, shift_ref, w_ref, o_ref, h_ref):
    @pl.when(pl.program_id(2) == 0)
    def _():
        h = _modulated_rmsnorm(x_ref[0], g_ref[...], scale_ref[0], shift_ref[0])
        h_ref[...] = h.astype(BF16)

    w = w_ref[...].astype(BF16)
    o_ref[0] = jnp.dot(h_ref[...], w, preferred_element_type=F32).astype(o_ref.dtype)


def _inproj(x, g, scale, shift, w):
    bsz, s, d = x.shape
    n = w.shape[1]
    tm, tn = INPROJ_TM, INPROJ_TN
    vmem = 2 * tm * d * 4 + 2 * d * tn * 4 + d * tn * 2 + 2 * tm * tn * 2 + tm * d * 2
    return pl.pallas_call(
        _inproj_kernel,
        grid=(bsz, s // tm, n // tn),
        in_specs=[pl.BlockSpec((1, tm, d), lambda b, i, j: (b, i, 0)),
                  pl.BlockSpec((1, d), lambda b, i, j: (0, 0)),
                  pl.BlockSpec((1, 1, d), lambda b, i, j: (b, 0, 0)),
                  pl.BlockSpec((1, 1, d), lambda b, i, j: (b, 0, 0)),
                  pl.BlockSpec((d, tn), lambda b, i, j: (0, j))],
        out_specs=pl.BlockSpec((1, tm, tn), lambda b, i, j: (b, i, j)),
        out_shape=jax.ShapeDtypeStruct((bsz, s, n), BF16),
        scratch_shapes=[pltpu.VMEM((tm, d), BF16)],
        compiler_params=pltpu.CompilerParams(
            dimension_semantics=("arbitrary", "arbitrary", "arbitrary"),
            vmem_limit_bytes=_vmem_limit(vmem)),
        name="inproj",
    )(x, g.reshape(1, d), scale, shift, w)


def _diff_attn_kernel(slopes_ref, q1_ref, q2_ref, k1_ref, k2_ref, v_ref,
                      lq1_ref, lk1_ref, lq2_ref, lk2_ref, g_ref, o_ref,
                      bias_ref, vt_ref, acc1_ref, acc2_ref, p1_ref, p2_ref, t2_ref, *, lam_init, bq, bk):
    h = pl.program_id(1)
    i = pl.program_id(2)
    slope_l2 = slopes_ref[h] * LOG2E
    n_kv = v_ref.shape[1] // bk
    ratio = bq // bk

    @pl.when(i == 0)
    def _():
        key = lax.broadcasted_iota(jnp.int32, (bk, bq), 0)
        qry = lax.broadcasted_iota(jnp.int32, (bk, bq), 1)
        bias = (key - qry).astype(F32) * slope_l2
        bias_ref[0] = bias
        for d in range(ratio):
            bias_ref[1 + d] = jnp.where(qry >= key + d * bk, bias, -jnp.inf)

        def xpose(c, _):
            start = pl.multiple_of(c * bk, bk)
            vt_ref[:, pl.ds(start, bk)] = v_ref[0, pl.ds(start, bk), :].T
            return 0
        lax.fori_loop(0, n_kv, xpose, 0)

    qscale = (HEAD_DIM ** -0.5) * LOG2E
    q1t = (q1_ref[0].astype(F32) * qscale).astype(BF16).T
    q2t = (q2_ref[0].astype(F32) * qscale).astype(BF16).T

    acc1_ref[...] = jnp.zeros_like(acc1_ref)
    acc2_ref[...] = jnp.zeros_like(acc2_ref)

    def scores(qt, k_ref, start):
        k = k_ref[0, pl.ds(start, bk), :]
        return jnp.dot(k, qt, preferred_element_type=F32)

    def tile_offset(j):
        return slope_l2 * (i * bq - j * bk).astype(F32)

    def bias_variant(j):
        d = j - ratio * i
        return jnp.where(d >= 0, d + 1, 0)

    def softmax(t, mx, off, m_old, l_old, p_ref):
        m_new = jnp.maximum(m_old, mx)
        p = jnp.exp2(t - (m_new + off))
        alpha = jnp.exp2(m_old - m_new)
        l_new = alpha * l_old + jnp.sum(p, axis=0, keepdims=True)
        p_ref[...] = p.astype(BF16)
        return m_new, l_new, alpha

    def softmax_now(s, sel, off, m_old, l_old, p_ref):
        t = s + bias_ref[sel]
        return softmax(t, jnp.max(t, axis=0, keepdims=True) - off, off, m_old, l_old, p_ref)

    def park_scores(s, sel, off):
        t = s + bias_ref[sel]
        t2_ref[...] = t
        return jnp.max(t, axis=0, keepdims=True) - off

    def softmax_parked(mx, off, m_old, l_old, p_ref):
        return softmax(t2_ref[...], mx, off, m_old, l_old, p_ref)

    def pv(vt, acc_ref, p_ref, alpha):
        acc_ref[...] = alpha * acc_ref[...] + jnp.dot(vt, p_ref[...], preferred_element_type=F32)

    neg = jnp.full((1, bq), -jnp.inf, F32)
    zero = jnp.zeros((1, bq), F32)
    last = ratio * i + ratio - 1

    sel0 = bias_variant(0)
    off0 = tile_offset(0)
    s1 = scores(q1t, k1_ref, 0)
    s2 = scores(q2t, k2_ref, 0)
    m1, l1, a1 = softmax_now(s1, sel0, off0, neg, zero, p1_ref)
    mx2 = park_scores(s2, sel0, off0)

    def step(j, carry):
        m1, l1, a1, m2, l2, mx2 = carry
        sel = bias_variant(j)
        start = pl.multiple_of(j * bk, bk)
        vt_prev = vt_ref[:, pl.ds(pl.multiple_of((j - 1) * bk, bk), bk)]
        off = tile_offset(j)
        s1 = scores(q1t, k1_ref, start)
        pv(vt_prev, acc1_ref, p1_ref, a1)
        m2, l2, a2 = softmax_parked(mx2, tile_offset(j - 1), m2, l2, p2_ref)
        s2 = scores(q2t, k2_ref, start)
        m1, l1, a1 = softmax_now(s1, sel, off, m1, l1, p1_ref)
        pv(vt_prev, acc2_ref, p2_ref, a2)
        mx2 = park_scores(s2, sel, off)
        return m1, l1, a1, m2, l2, mx2

    m1, l1, a1, m2, l2, mx2 = lax.fori_loop(1, last + 1, step, (m1, l1, a1, neg, zero, mx2))

    vt_last = vt_ref[:, pl.ds(pl.multiple_of(last * bk, bk), bk)]
    pv(vt_last, acc1_ref, p1_ref, a1)
    m2, l2, a2 = softmax_parked(mx2, tile_offset(last), m2, l2, p2_ref)
    pv(vt_last, acc2_ref, p2_ref, a2)

    lam = (jnp.exp(jnp.sum(lq1_ref[...] * lk1_ref[...], axis=-1, keepdims=True))
           - jnp.exp(jnp.sum(lq2_ref[...] * lk2_ref[...], axis=-1, keepdims=True))
           + lam_init)
    yt = acc1_ref[...] * (1.0 / l1) - acc2_ref[...] * (lam / l2)
    ynt = yt * lax.rsqrt(jnp.mean(yt * yt, axis=0, keepdims=True) + NORM_EPS)
    o_ref[0] = (ynt * (g_ref[...] * (1.0 - lam_init))).astype(o_ref.dtype).T


def _diff_attention(proj, slopes, lq1, lk1, lq2, lk2, subln_g, *, n_heads, d_model, lam_init):
    bsz, s, _ = proj.shape
    bq, bk = ATTN_BQ, ATTN_BK
    dv = d_model // n_heads
    k_col0 = (n_heads * 2 * HEAD_DIM) // HEAD_DIM
    v_col0 = (2 * n_heads * 2 * HEAD_DIM) // dv
    qspec = lambda m: pl.BlockSpec((1, bq, HEAD_DIM), lambda b, h, i, sl: (b, i, 2 * h + m))
    kspec = lambda m: pl.BlockSpec((1, s, HEAD_DIM), lambda b, h, i, sl: (b, 0, k_col0 + 2 * h + m))
    vec = pl.BlockSpec((1, HEAD_DIM), lambda b, h, i, sl: (0, 0))
    n_bias = 1 + bq // bk
    vmem = (2 * 2 * s * HEAD_DIM * 2 + 3 * s * dv * 2 + (n_bias + 1) * bk * bq * 4 + 2 * bq * dv * 4
            + 2 * bk * bq * 2 + 6 * bk * bq * 4)
    kernel = functools.partial(_diff_attn_kernel, lam_init=lam_init, bq=bq, bk=bk)
    return pl.pallas_call(
        kernel,
        grid_spec=pltpu.PrefetchScalarGridSpec(
            num_scalar_prefetch=1,
            grid=(bsz, n_heads, s // bq),
            in_specs=[qspec(0), qspec(1), kspec(0), kspec(1),
                      pl.BlockSpec((1, s, dv), lambda b, h, i, sl: (b, 0, v_col0 + h)),
                      vec, vec, vec, vec,
                      pl.BlockSpec((dv, 1), lambda b, h, i, sl: (h, 0))],
            out_specs=pl.BlockSpec((1, bq, dv), lambda b, h, i, sl: (b, i, h)),
            scratch_shapes=[pltpu.VMEM((n_bias, bk, bq), F32),
                            pltpu.VMEM((dv, s), BF16),
                            pltpu.VMEM((dv, bq), F32), pltpu.VMEM((dv, bq), F32),
                            pltpu.VMEM((bk, bq), BF16), pltpu.VMEM((bk, bq), BF16),
                            pltpu.VMEM((bk, bq), F32)]),
        out_shape=jax.ShapeDtypeStruct((bsz, s, d_model), BF16),
        compiler_params=pltpu.CompilerParams(
            dimension_semantics=("arbitrary", "arbitrary", "arbitrary"),
            vmem_limit_bytes=_vmem_limit(vmem)),
        name="diff_attention",
    )(slopes, proj, proj, proj, proj, proj, lq1, lk1, lq2, lk2, subln_g.reshape(d_model, 1))


def _retention_merge_kernel(logg_ref, q_ref, k_ref, v_ref, rg_ref, ga_ref, gb_ref, ya_ref,
                            gnw_ref, gnb_ref, o_ref, decay_ref, state_ref, xi_ref, zeta_ref, *, blk, dv):
    hp = pl.program_id(1)
    n = pl.program_id(2)
    kscale = HEAD_DIM ** -0.5
    heads = range(RET_HEADS)
    log_gs = [logg_ref[hp * RET_HEADS + hh] for hh in heads]

    @pl.when(n == 0)
    def _():
        row = lax.broadcasted_iota(jnp.int32, (blk, blk), 0)
        col = lax.broadcasted_iota(jnp.int32, (blk, blk), 1)
        rel = (row - col).astype(F32)
        idx = lax.broadcasted_iota(jnp.int32, (blk, 1), 0).astype(F32)
        for hh in heads:
            decay_ref[hh] = jnp.where(rel >= 0, jnp.exp(log_gs[hh] * jnp.maximum(rel, 0.0)) * kscale, 0.0)
            xi_ref[hh] = jnp.exp(log_gs[hh] * (idx + 1.0))
            zeta_ref[hh] = jnp.exp(log_gs[hh] * (blk - 1.0 - idx)) * kscale
        state_ref[...] = jnp.zeros_like(state_ref)

    def mix(hh):
        log_g = log_gs[hh]
        qk = slice(hh * HEAD_DIM, (hh + 1) * HEAD_DIM)
        vv = slice(hh * dv, (hh + 1) * dv)
        q = q_ref[0, :, qk]
        k = k_ref[0, :, qk]
        v = v_ref[0, :, vv]
        s = lax.dot_general(q, k, (((1,), (1,)), ((), ())), preferred_element_type=F32)
        intra = jnp.dot((s * decay_ref[hh]).astype(BF16), v, preferred_element_type=F32)
        state = state_ref[hh]
        cross = jnp.dot((q.astype(F32) * xi_ref[hh]).astype(BF16), state.astype(BF16),
                        preferred_element_type=F32)
        kz = (k.astype(F32) * zeta_ref[hh]).astype(BF16)
        kv = lax.dot_general(kz, v, (((0,), (0,)), ((), ())), preferred_element_type=F32)
        state_ref[hh] = jnp.exp(log_g * blk) * state + kv
        return intra + cross

    def finish(hh, y):
        vv = slice(hh * dv, (hh + 1) * dv)
        mu = jnp.mean(y, axis=-1, keepdims=True)
        yc = y - mu
        var = jnp.mean(yc * yc, axis=-1, keepdims=True)
        gn = yc * lax.rsqrt(var + NORM_EPS) * gnw_ref[:, vv] + gnb_ref[:, vv]
        rg = rg_ref[0, :, vv].astype(F32)
        yb = rg * _sigmoid(rg) * gn
        merged = (_sigmoid(ga_ref[0, :, vv].astype(F32)) * ya_ref[0, :, vv].astype(F32)
                  + _sigmoid(gb_ref[0, :, vv].astype(F32)) * yb)
        o_ref[0, :, vv] = merged.astype(o_ref.dtype)

    ys = [mix(hh) for hh in heads]
    for hh in heads:
        finish(hh, ys[hh])


def _retention_merge(proj, ya, log_g, gn_w, gn_b, *, n_heads, d_model, col0):
    bsz, s, _ = proj.shape
    blk = RET_BLOCK
    dv = d_model // n_heads
    nw = RET_HEADS * HEAD_DIM
    ww = RET_HEADS * dv
    groups = n_heads // RET_HEADS
    q0 = col0 // nw
    k0 = q0 + groups
    v0 = (col0 + 2 * n_heads * HEAD_DIM) // ww
    rg0 = v0 + groups
    ga0 = rg0 + groups
    gb0 = ga0 + groups
    wide = lambda c0: pl.BlockSpec((1, blk, ww), lambda b, h, n, lg: (b, n, c0 + h))
    narrow = lambda c0: pl.BlockSpec((1, blk, nw), lambda b, h, n, lg: (b, n, c0 + h))
    chan = pl.BlockSpec((1, ww), lambda b, h, n, lg: (0, h))
    vmem = 2 * (2 * blk * nw * 2 + 6 * blk * ww * 2) + RET_HEADS * (blk * blk * 4 + HEAD_DIM * dv * 4) \
        + 4 * RET_HEADS * blk * blk * 4
    kernel = functools.partial(_retention_merge_kernel, blk=blk, dv=dv)
    return pl.pallas_call(
        kernel,
        grid_spec=pltpu.PrefetchScalarGridSpec(
            num_scalar_prefetch=1,
            grid=(bsz, groups, s // blk),
            in_specs=[narrow(q0), narrow(k0), wide(v0), wide(rg0), wide(ga0), wide(gb0),
                      wide(0), chan, chan],
            out_specs=wide(0),
            scratch_shapes=[pltpu.VMEM((RET_HEADS, blk, blk), F32),
                            pltpu.VMEM((RET_HEADS, HEAD_DIM, dv), F32),
                            pltpu.VMEM((RET_HEADS, blk, 1), F32), pltpu.VMEM((RET_HEADS, blk, 1), F32)]),
        out_shape=jax.ShapeDtypeStruct((bsz, s, d_model), BF16),
        compiler_params=pltpu.CompilerParams(
            dimension_semantics=("arbitrary", "arbitrary", "arbitrary"),
            vmem_limit_bytes=_vmem_limit(vmem)),
        name="retention_merge",
    )(log_g, proj, proj, proj, proj, proj, proj, ya, gn_w.reshape(1, d_model), gn_b.reshape(1, d_model))


def _outproj_kernel(m_ref, w_ref, x_ref, gate_ref, g_ref, scale_ref, shift_ref, x1_ref, h2_ref):
    half = m_ref.shape[1] // 2
    rows = [slice(0, half), slice(half, 2 * half)]
    ys = [jnp.dot(m_ref[0, r, :], w_ref[...], preferred_element_type=F32) for r in rows]
    for r, y in zip(rows, ys):
        x1 = x_ref[0, r, :] + gate_ref[0] * y
        x1_ref[0, r, :] = x1
        h2_ref[0, r, :] = _modulated_rmsnorm(x1, g_ref[...], scale_ref[0],
                                             shift_ref[0]).astype(h2_ref.dtype)


def _outproj(merged, w_bf16, x, gate, g, scale, shift):
    bsz, s, d = x.shape
    tm = OUTPROJ_TM
    rows = pl.BlockSpec((1, tm, d), lambda b, i: (b, i, 0))
    mod = pl.BlockSpec((1, 1, d), lambda b, i: (b, 0, 0))
    vmem = 2 * (tm * d * 2 + d * d * 2 + tm * d * 4 + tm * d * 4 + tm * d * 2)
    return pl.pallas_call(
        _outproj_kernel,
        grid=(bsz, s // tm),
        in_specs=[rows, pl.BlockSpec((d, d), lambda b, i: (0, 0)), rows, mod,
                  pl.BlockSpec((1, d), lambda b, i: (0, 0)), mod, mod],
        out_specs=[rows, rows],
        out_shape=[jax.ShapeDtypeStruct((bsz, s, d), F32), jax.ShapeDtypeStruct((bsz, s, d), BF16)],
        compiler_params=pltpu.CompilerParams(
            dimension_semantics=("arbitrary", "arbitrary"),
            vmem_limit_bytes=_vmem_limit(vmem)),
        name="outproj",
    )(merged, w_bf16, x, gate, g.reshape(1, d), scale, shift)


def _ffn_kernel(h_ref, wg_ref, wu_ref, wd_ref, x1_ref, gate_ref, fg_ref, o_ref, acc_ref):
    f = pl.program_id(2)

    @pl.when(f == 0)
    def _():
        acc_ref[...] = jnp.zeros_like(acc_ref)

    h = h_ref[0]
    g = jnp.dot(h, wg_ref[...], preferred_element_type=F32)
    u = jnp.dot(h, wu_ref[...], preferred_element_type=F32)
    a = (g * _sigmoid(g) * u).astype(BF16)
    acc_ref[...] += jnp.dot(a, wd_ref[...], preferred_element_type=F32)

    @pl.when(f == pl.num_programs(2) - 1)
    def _():
        x2 = x1_ref[0] + gate_ref[0] * acc_ref[...]
        ms = jnp.mean(x2 * x2, axis=-1, keepdims=True)
        o_ref[0] = x2 * lax.rsqrt(ms + NORM_EPS) * fg_ref[...]


def _ffn(h2, wg, wu, wd, x1, gate, final_g):
    bsz, s, d = x1.shape
    dff = wg.shape[1]
    tm, tf = FFN_TM, FFN_TF
    rows = lambda: pl.BlockSpec((1, tm, d), lambda b, i, f: (b, i, 0))
    vmem = 2 * (tm * d * 2 + 2 * d * tf * 2 + tf * d * 2 + tm * d * 4 + tm * d * 4) + tm * d * 4
    return pl.pallas_call(
        _ffn_kernel,
        grid=(bsz, s // tm, dff // tf),
        in_specs=[rows(),
                  pl.BlockSpec((d, tf), lambda b, i, f: (0, f)),
                  pl.BlockSpec((d, tf), lambda b, i, f: (0, f)),
                  pl.BlockSpec((tf, d), lambda b, i, f: (f, 0)),
                  rows(),
                  pl.BlockSpec((1, 1, d), lambda b, i, f: (b, 0, 0)),
                  pl.BlockSpec((1, d), lambda b, i, f: (0, 0))],
        out_specs=rows(),
        out_shape=jax.ShapeDtypeStruct((bsz, s, d), F32),
        scratch_shapes=[pltpu.VMEM((tm, d), F32)],
        compiler_params=pltpu.CompilerParams(
            dimension_semantics=("arbitrary", "arbitrary", "arbitrary"),
            vmem_limit_bytes=_vmem_limit(vmem)),
        name="ffn",
    )(h2, wg, wu, wd, x1, gate, final_g.reshape(1, d))


def kernel(x, c, w_ada, b_ada, norm1_g, w_in, lam_q1, lam_k1, lam_q2, lam_k2, diff_subln_g,
           ret_gn_w, ret_gn_b, w_out, norm2_g, w_ffn_gate, w_ffn_up, w_ffn_down, final_g):
    bsz, s, d = x.shape
    depth = w_ada.shape[0]
    assert depth == 1, "the final RMSNorm is fused into the FFN kernel of the only layer"
    n_heads = d // (2 * HEAD_DIM)
    diff_qk_w = n_heads * 2 * HEAD_DIM
    ret_col0 = 2 * diff_qk_w + d
    slopes = jnp.asarray([2.0 ** (-8.0 * (h + 1) / n_heads) for h in range(n_heads)], F32)
    log_g = jnp.asarray([math.log(1.0 - 2.0 ** (-5 - h)) for h in range(n_heads)], F32)

    for l in range(depth):
        lam_init = 0.8 - 0.6 * math.exp(-0.3 * l)
        mod = _adaln(c, w_ada[l], b_ada[l])
        shift1, scale1, gate1, shift2, scale2, gate2 = (
            m.reshape(bsz, 1, d) for m in jnp.split(mod, 6, axis=-1))
        proj = _inproj(x, norm1_g[l], scale1, shift1, w_in[l])
        ya = _diff_attention(proj, slopes, lam_q1[l:l + 1], lam_k1[l:l + 1], lam_q2[l:l + 1],
                             lam_k2[l:l + 1], diff_subln_g[l], n_heads=n_heads, d_model=d,
                             lam_init=lam_init)
        merged = _retention_merge(proj, ya, log_g, ret_gn_w[l], ret_gn_b[l],
                                  n_heads=n_heads, d_model=d, col0=ret_col0)
        x1, h2 = _outproj(merged, w_out[l].astype(BF16), x, gate1, norm2_g[l], scale2, shift2)
        x = _ffn(h2, w_ffn_gate[l].astype(BF16), w_ffn_up[l].astype(BF16),
                 w_ffn_down[l].astype(BF16), x1, gate2, final_g)
    return x
```

```python
import functools
import math

import jax
import jax.numpy as jnp
from jax import lax
from jax.experimental import pallas as pl
from jax.experimental.pallas import tpu as pltpu

F32 = jnp.float32
BF16 = jnp.bfloat16

HEAD_DIM = 128
NORM_EPS = 1e-6
LOG2E = math.log2(math.e)

V7X_VMEM_BYTES = 64 * 1024 * 1024
V7X_SUBLANES = 8

ADA_TN = 1024
INPROJ_TM = 1024
INPROJ_TN = 1024
ATTN_BQ = 512
ATTN_BK = 512
ATTN_STRIP = 256
RET_BLOCK = 512
RET_HEADS = 4
OUTPROJ_TM = 512
FFN_TM = 512
FFN_TF = 512


def _vmem_limit(nbytes):
    return int(min(nbytes + 16 * 1024 * 1024, V7X_VMEM_BYTES - 4 * 1024 * 1024))


def _sigmoid(x):
    return 1.0 / (1.0 + jnp.exp(-x))


def _adaln_kernel(c_ref, w_ref, b_ref, o_ref):
    c = c_ref[...]
    sc = c * _sigmoid(c)
    o_ref[...] = jnp.dot(sc, w_ref[...], preferred_element_type=F32,
                         precision=lax.Precision.HIGHEST) + b_ref[...]


def _adaln(c, w, b):
    bsz, d = c.shape
    n = w.shape[1]
    rows = -(-bsz // V7X_SUBLANES) * V7X_SUBLANES
    c_pad = jnp.pad(c, ((0, rows - bsz), (0, 0)))
    out = pl.pallas_call(
        _adaln_kernel,
        grid=(n // ADA_TN,),
        in_specs=[pl.BlockSpec((rows, d), lambda j: (0, 0)),
                  pl.BlockSpec((d, ADA_TN), lambda j: (0, j)),
                  pl.BlockSpec((1, ADA_TN), lambda j: (0, j))],
        out_specs=pl.BlockSpec((rows, ADA_TN), lambda j: (0, j)),
        out_shape=jax.ShapeDtypeStruct((rows, n), F32),
        compiler_params=pltpu.CompilerParams(
            dimension_semantics=("arbitrary",),
            vmem_limit_bytes=_vmem_limit(2 * d * ADA_TN * 4)),
        name="adaln_mod",
    )(c_pad, w, b.reshape(1, n))
    return out[:bsz]


def _modulated_rmsnorm(x, g, scale, shift):
    ms = jnp.mean(x * x, axis=-1, keepdims=True)
    y = x * lax.rsqrt(ms + NORM_EPS) * g
    return y * (1.0 + scale) + shift


def _inproj_kernel(x_ref, g_ref, scale_ref, shift_ref, w_ref, o_ref, h_ref):
    @pl.when(pl.program_id(2) == 0)
    def _():
        h = _modulated_rmsnorm(x_ref[0], g_ref[...], scale_ref[0], shift_ref[0])
        h_ref[...] = h.astype(BF16)

    w = w_ref[...].astype(BF16)
    o_ref[0] = jnp.dot(h_ref[...], w, preferred_element_type=F32).astype(o_ref.dtype)


def _inproj(x, g, scale, shift, w):
    bsz, s, d = x.shape
    n = w.shape[1]
    tm, tn = INPROJ_TM, INPROJ_TN
    vmem = 2 * tm * d * 4 + 2 * d * tn * 4 + d * tn * 2 + 2 * tm * tn * 2 + tm * d * 2
    return pl.pallas_call(
        _inproj_kernel,
        grid=(bsz, s // tm, n // tn),
        in_specs=[pl.BlockSpec((1, tm, d), lambda b, i, j: (b, i, 0)),
                  pl.BlockSpec((1, d), lambda b, i, j: (0, 0)),
                  pl.BlockSpec((1, 1, d), lambda b, i, j: (b, 0, 0)),
                  pl.BlockSpec((1, 1, d), lambda b, i, j: (b, 0, 0)),
                  pl.BlockSpec((d, tn), lambda b, i, j: (0, j))],
        out_specs=pl.BlockSpec((1, tm, tn), lambda b, i, j: (b, i, j)),
        out_shape=jax.ShapeDtypeStruct((bsz, s, n), BF16),
        scratch_shapes=[pltpu.VMEM((tm, d), BF16)],
        compiler_params=pltpu.CompilerParams(
            dimension_semantics=("arbitrary", "arbitrary", "arbitrary"),
            vmem_limit_bytes=_vmem_limit(vmem)),
        name="inproj",
    )(x, g.reshape(1, d), scale, shift, w)


def _diff_attn_kernel(slopes_ref, q1_ref, q2_ref, k1_ref, k2_ref, v_ref,
                      lq1_ref, lk1_ref, lq2_ref, lk2_ref, g_ref, o_ref,
                      bias_ref, vt_ref, acc1_ref, acc2_ref, p1_ref, p2_ref, t2_ref, *, lam_init, bq, bk):
    h = pl.program_id(1)
    i = pl.program_id(2)
    slope_l2 = slopes_ref[h] * LOG2E
    n_kv = v_ref.shape[1] // bk
    ratio = bq // bk

    @pl.when(i == 0)
    def _():
        key = lax.broadcasted_iota(jnp.int32, (bk, bq), 0)
        qry = lax.broadcasted_iota(jnp.int32, (bk, bq), 1)
        bias = (key - qry).astype(F32) * slope_l2
        bias_ref[0] = bias
        for d in range(ratio):
            bias_ref[1 + d] = jnp.where(qry >= key + d * bk, bias, -jnp.inf)

        def xpose(c, _):
            start = pl.multiple_of(c * bk, bk)
            vt_ref[:, pl.ds(start, bk)] = v_ref[0, pl.ds(start, bk), :].T
            return 0
        lax.fori_loop(0, n_kv, xpose, 0)

    qscale = (HEAD_DIM ** -0.5) * LOG2E
    q1t = (q1_ref[0].astype(F32) * qscale).astype(BF16).T
    q2t = (q2_ref[0].astype(F32) * qscale).astype(BF16).T

    acc1_ref[...] = jnp.zeros_like(acc1_ref)
    acc2_ref[...] = jnp.zeros_like(acc2_ref)
    strips = [slice(c, c + ATTN_STRIP) for c in range(0, bq, ATTN_STRIP)]
    cat = lambda xs: jnp.concatenate(xs, axis=1)

    def scores(qt, k_ref, start):
        k = k_ref[0, pl.ds(start, bk), :]
        return [jnp.dot(k, qt[:, cs], preferred_element_type=F32) for cs in strips]

    def tile_offset(j):
        return slope_l2 * (i * bq - j * bk).astype(F32)

    def bias_variant(j):
        d = j - ratio * i
        return jnp.where(d >= 0, d + 1, 0)

    def softmax(ts, mxs, off, m_old, l_old, p_ref):
        m_out, l_out, a_out = [], [], []
        for cs, t, mx in zip(strips, ts, mxs):
            m_new = jnp.maximum(m_old[:, cs], mx)
            p = jnp.exp2(t() - (m_new + off))
            alpha = jnp.exp2(m_old[:, cs] - m_new)
            l_out.append(alpha * l_old[:, cs] + jnp.sum(p, axis=0, keepdims=True))
            m_out.append(m_new)
            a_out.append(alpha)
            p_ref[:, cs] = p.astype(BF16)
        return cat(m_out), cat(l_out), cat(a_out)

    def softmax_now(ss, sel, off, m_old, l_old, p_ref):
        ts = [s + bias_ref[sel, :, cs] for cs, s in zip(strips, ss)]
        mxs = [jnp.max(t, axis=0, keepdims=True) - off for t in ts]
        return softmax([lambda t=t: t for t in ts], mxs, off, m_old, l_old, p_ref)

    def park_scores(ss, sel, off):
        mxs = []
        for cs, s in zip(strips, ss):
            t = s + bias_ref[sel, :, cs]
            t2_ref[:, cs] = t
            mxs.append(jnp.max(t, axis=0, keepdims=True) - off)
        return cat(mxs)

    def softmax_parked(mx, off, m_old, l_old, p_ref):
        ts = [lambda cs=cs: t2_ref[:, cs] for cs in strips]
        return softmax(ts, [mx[:, cs] for cs in strips], off, m_old, l_old, p_ref)

    def pv(vt, acc_ref, p_ref, alpha):
        acc_ref[...] = alpha * acc_ref[...] + jnp.dot(vt, p_ref[...], preferred_element_type=F32)

    neg = jnp.full((1, bq), -jnp.inf, F32)
    zero = jnp.zeros((1, bq), F32)
    last = ratio * i + ratio - 1

    sel0 = bias_variant(0)
    off0 = tile_offset(0)
    s1 = scores(q1t, k1_ref, 0)
    s2 = scores(q2t, k2_ref, 0)
    m1, l1, a1 = softmax_now(s1, sel0, off0, neg, zero, p1_ref)
    mx2 = park_scores(s2, sel0, off0)

    def step(j, carry):
        m1, l1, a1, m2, l2, mx2 = carry
        sel = bias_variant(j)
        start = pl.multiple_of(j * bk, bk)
        vt_prev = vt_ref[:, pl.ds(pl.multiple_of((j - 1) * bk, bk), bk)]
        off = tile_offset(j)
        s1 = scores(q1t, k1_ref, start)
        pv(vt_prev, acc1_ref, p1_ref, a1)
        m2, l2, a2 = softmax_parked(mx2, tile_offset(j - 1), m2, l2, p2_ref)
        s2 = scores(q2t, k2_ref, start)
        m1, l1, a1 = softmax_now(s1, sel, off, m1, l1, p1_ref)
        pv(vt_prev, acc2_ref, p2_ref, a2)
        mx2 = park_scores(s2, sel, off)
        return m1, l1, a1, m2, l2, mx2

    m1, l1, a1, m2, l2, mx2 = lax.fori_loop(1, last + 1, step, (m1, l1, a1, neg, zero, mx2))

    vt_last = vt_ref[:, pl.ds(pl.multiple_of(last * bk, bk), bk)]
    pv(vt_last, acc1_ref, p1_ref, a1)
    m2, l2, a2 = softmax_parked(mx2, tile_offset(last), m2, l2, p2_ref)
    pv(vt_last, acc2_ref, p2_ref, a2)

    lam = (jnp.exp(jnp.sum(lq1_ref[...] * lk1_ref[...], axis=-1, keepdims=True))
           - jnp.exp(jnp.sum(lq2_ref[...] * lk2_ref[...], axis=-1, keepdims=True))
           + lam_init)
    yt = acc1_ref[...] * (1.0 / l1) - acc2_ref[...] * (lam / l2)
    ynt = yt * lax.rsqrt(jnp.mean(yt * yt, axis=0, keepdims=True) + NORM_EPS)
    o_ref[0] = (ynt * (g_ref[...] * (1.0 - lam_init))).astype(o_ref.dtype).T


def _diff_attention(proj, slopes, lq1, lk1, lq2, lk2, subln_g, *, n_heads, d_model, lam_init):
    bsz, s, _ = proj.shape
    bq, bk = ATTN_BQ, ATTN_BK
    dv = d_model // n_heads
    k_col0 = (n_heads * 2 * HEAD_DIM) // HEAD_DIM
    v_col0 = (2 * n_heads * 2 * HEAD_DIM) // dv
    qspec = lambda m: pl.BlockSpec((1, bq, HEAD_DIM), lambda b, h, i, sl: (b, i, 2 * h + m))
    kspec = lambda m: pl.BlockSpec((1, s, HEAD_DIM), lambda b, h, i, sl: (b, 0, k_col0 + 2 * h + m))
    vec = pl.BlockSpec((1, HEAD_DIM), lambda b, h, i, sl: (0, 0))
    n_bias = 1 + bq // bk
    vmem = (2 * 2 * s * HEAD_DIM * 2 + 3 * s * dv * 2 + (n_bias + 1) * bk * bq * 4 + 2 * bq * dv * 4
            + 2 * bk * bq * 2 + 6 * bk * bq * 4)
    kernel = functools.partial(_diff_attn_kernel, lam_init=lam_init, bq=bq, bk=bk)
    return pl.pallas_call(
        kernel,
        grid_spec=pltpu.PrefetchScalarGridSpec(
            num_scalar_prefetch=1,
            grid=(bsz, n_heads, s // bq),
            in_specs=[qspec(0), qspec(1), kspec(0), kspec(1),
                      pl.BlockSpec((1, s, dv), lambda b, h, i, sl: (b, 0, v_col0 + h)),
                      vec, vec, vec, vec,
                      pl.BlockSpec((dv, 1), lambda b, h, i, sl: (h, 0))],
            out_specs=pl.BlockSpec((1, bq, dv), lambda b, h, i, sl: (b, i, h)),
            scratch_shapes=[pltpu.VMEM((n_bias, bk, bq), F32),
                            pltpu.VMEM((dv, s), BF16),
                            pltpu.VMEM((dv, bq), F32), pltpu.VMEM((dv, bq), F32),
                            pltpu.VMEM((bk, bq), BF16), pltpu.VMEM((bk, bq), BF16),
                            pltpu.VMEM((bk, bq), F32)]),
        out_shape=jax.ShapeDtypeStruct((bsz, s, d_model), BF16),
        compiler_params=pltpu.CompilerParams(
            dimension_semantics=("arbitrary", "arbitrary", "arbitrary"),
            vmem_limit_bytes=_vmem_limit(vmem)),
        name="diff_attention",
    )(slopes, proj, proj, proj, proj, proj, lq1, lk1, lq2, lk2, subln_g.reshape(d_model, 1))


def _retention_merge_kernel(logg_ref, q_ref, k_ref, v_ref, rg_ref, ga_ref, gb_ref, ya_ref,
                            gnw_ref, gnb_ref, o_ref, decay_ref, state_ref, xi_ref, zeta_ref, *, blk, dv):
    hp = pl.program_id(1)
    n = pl.program_id(2)
    kscale = HEAD_DIM ** -0.5
    heads = range(RET_HEADS)
    log_gs = [logg_ref[hp * RET_HEADS + hh] for hh in heads]

    @pl.when(n == 0)
    def _():
        row = lax.broadcasted_iota(jnp.int32, (blk, blk), 0)
        col = lax.broadcasted_iota(jnp.int32, (blk, blk), 1)
        rel = (row - col).astype(F32)
        idx = lax.broadcasted_iota(jnp.int32, (blk, 1), 0).astype(F32)
        for hh in heads:
            decay_ref[hh] = jnp.where(rel >= 0, jnp.exp(log_gs[hh] * jnp.maximum(rel, 0.0)) * kscale, 0.0)
            xi_ref[hh] = jnp.exp(log_gs[hh] * (idx + 1.0))
            zeta_ref[hh] = jnp.exp(log_gs[hh] * (blk - 1.0 - idx)) * kscale
        state_ref[...] = jnp.zeros_like(state_ref)

    def mix(hh):
        log_g = log_gs[hh]
        qk = slice(hh * HEAD_DIM, (hh + 1) * HEAD_DIM)
        vv = slice(hh * dv, (hh + 1) * dv)
        q = q_ref[0, :, qk]
        k = k_ref[0, :, qk]
        v = v_ref[0, :, vv]
        s = lax.dot_general(q, k, (((1,), (1,)), ((), ())), preferred_element_type=F32)
        intra = jnp.dot((s * decay_ref[hh]).astype(BF16), v, preferred_element_type=F32)
        state = state_ref[hh]
        cross = jnp.dot((q.astype(F32) * xi_ref[hh]).astype(BF16), state.astype(BF16),
                        preferred_element_type=F32)
        kz = (k.astype(F32) * zeta_ref[hh]).astype(BF16)
        kv = lax.dot_general(kz, v, (((0,), (0,)), ((), ())), preferred_element_type=F32)
        state_ref[hh] = jnp.exp(log_g * blk) * state + kv
        return intra + cross

    def finish(hh, y):
        vv = slice(hh * dv, (hh + 1) * dv)
        mu = jnp.mean(y, axis=-1, keepdims=True)
        yc = y - mu
        var = jnp.mean(yc * yc, axis=-1, keepdims=True)
        gn = yc * lax.rsqrt(var + NORM_EPS) * gnw_ref[:, vv] + gnb_ref[:, vv]
        rg = rg_ref[0, :, vv].astype(F32)
        yb = rg * _sigmoid(rg) * gn
        merged = (_sigmoid(ga_ref[0, :, vv].astype(F32)) * ya_ref[0, :, vv].astype(F32)
                  + _sigmoid(gb_ref[0, :, vv].astype(F32)) * yb)
        o_ref[0, :, vv] = merged.astype(o_ref.dtype)

    ys = [mix(hh) for hh in heads]
    for hh in heads:
        finish(hh, ys[hh])


def _retention_merge(proj, ya, log_g, gn_w, gn_b, *, n_heads, d_model, col0):
    bsz, s, _ = proj.shape
    blk = RET_BLOCK
    dv = d_model // n_heads
    nw = RET_HEADS * HEAD_DIM
    ww = RET_HEADS * dv
    groups = n_heads // RET_HEADS
    q0 = col0 // nw
    k0 = q0 + groups
    v0 = (col0 + 2 * n_heads * HEAD_DIM) // ww
    rg0 = v0 + groups
    ga0 = rg0 + groups
    gb0 = ga0 + groups
    wide = lambda c0: pl.BlockSpec((1, blk, ww), lambda b, h, n, lg: (b, n, c0 + h))
    narrow = lambda c0: pl.BlockSpec((1, blk, nw), lambda b, h, n, lg: (b, n, c0 + h))
    chan = pl.BlockSpec((1, ww), lambda b, h, n, lg: (0, h))
    vmem = 2 * (2 * blk * nw * 2 + 6 * blk * ww * 2) + RET_HEADS * (blk * blk * 4 + HEAD_DIM * dv * 4) \
        + 4 * RET_HEADS * blk * blk * 4
    kernel = functools.partial(_retention_merge_kernel, blk=blk, dv=dv)
    return pl.pallas_call(
        kernel,
        grid_spec=pltpu.PrefetchScalarGridSpec(
            num_scalar_prefetch=1,
            grid=(bsz, groups, s // blk),
            in_specs=[narrow(q0), narrow(k0), wide(v0), wide(rg0), wide(ga0), wide(gb0),
                      wide(0), chan, chan],
            out_specs=wide(0),
            scratch_shapes=[pltpu.VMEM((RET_HEADS, blk, blk), F32),
                            pltpu.VMEM((RET_HEADS, HEAD_DIM, dv), F32),
                            pltpu.VMEM((RET_HEADS, blk, 1), F32), pltpu.VMEM((RET_HEADS, blk, 1), F32)]),
        out_shape=jax.ShapeDtypeStruct((bsz, s, d_model), BF16),
        compiler_params=pltpu.CompilerParams(
            dimension_semantics=("arbitrary", "arbitrary", "arbitrary"),
            vmem_limit_bytes=_vmem_limit(vmem)),
        name="retention_merge",
    )(log_g, proj, proj, proj, proj, proj, proj, ya, gn_w.reshape(1, d_model), gn_b.reshape(1, d_model))


def _outproj_kernel(m_ref, w_ref, x_ref, gate_ref, g_ref, scale_ref, shift_ref, x1_ref, h2_ref):
    half = m_ref.shape[1] // 2
    rows = [slice(0, half), slice(half, 2 * half)]
    ys = [jnp.dot(m_ref[0, r, :], w_ref[...], preferred_element_type=F32) for r in rows]
    for r, y in zip(rows, ys):
        x1 = x_ref[0, r, :] + gate_ref[0] * y
        x1_ref[0, r, :] = x1
        h2_ref[0, r, :] = _modulated_rmsnorm(x1, g_ref[...], scale_ref[0],
                                             shift_ref[0]).astype(h2_ref.dtype)


def _outproj(merged, w_bf16, x, gate, g, scale, shift):
    bsz, s, d = x.shape
    tm = OUTPROJ_TM
    rows = pl.BlockSpec((1, tm, d), lambda b, i: (b, i, 0))
    mod = pl.BlockSpec((1, 1, d), lambda b, i: (b, 0, 0))
    vmem = 2 * (tm * d * 2 + d * d * 2 + tm * d * 4 + tm * d * 4 + tm * d * 2)
    return pl.pallas_call(
        _outproj_kernel,
        grid=(bsz, s // tm),
        in_specs=[rows, pl.BlockSpec((d, d), lambda b, i: (0, 0)), rows, mod,
                  pl.BlockSpec((1, d), lambda b, i: (0, 0)), mod, mod],
        out_specs=[rows, rows],
        out_shape=[jax.ShapeDtypeStruct((bsz, s, d), F32), jax.ShapeDtypeStruct((bsz, s, d), BF16)],
        compiler_params=pltpu.CompilerParams(
            dimension_semantics=("arbitrary", "arbitrary"),
            vmem_limit_bytes=_vmem_limit(vmem)),
        name="outproj",
    )(merged, w_bf16, x, gate, g.reshape(1, d), scale, shift)


def _ffn_kernel(h_ref, wg_ref, wu_ref, wd_ref, x1_ref, gate_ref, fg_ref, o_ref, acc_ref):
    f = pl.program_id(2)

    @pl.when(f == 0)
    def _():
        acc_ref[...] = jnp.zeros_like(acc_ref)

    h = h_ref[0]
    g = jnp.dot(h, wg_ref[...], preferred_element_type=F32)
    u = jnp.dot(h, wu_ref[...], preferred_element_type=F32)
    a = (g * _sigmoid(g) * u).astype(BF16)
    acc_ref[...] += jnp.dot(a, wd_ref[...], preferred_element_type=F32)

    @pl.when(f == pl.num_programs(2) - 1)
    def _():
        x2 = x1_ref[0] + gate_ref[0] * acc_ref[...]
        ms = jnp.mean(x2 * x2, axis=-1, keepdims=True)
        o_ref[0] = x2 * lax.rsqrt(ms + NORM_EPS) * fg_ref[...]


def _ffn(h2, wg, wu, wd, x1, gate, final_g):
    bsz, s, d = x1.shape
    dff = wg.shape[1]
    tm, tf = FFN_TM, FFN_TF
    rows = lambda: pl.BlockSpec((1, tm, d), lambda b, i, f: (b, i, 0))
    vmem = 2 * (tm * d * 2 + 2 * d * tf * 2 + tf * d * 2 + tm * d * 4 + tm * d * 4) + tm * d * 4
    return pl.pallas_call(
        _ffn_kernel,
        grid=(bsz, s // tm, dff // tf),
        in_specs=[rows(),
                  pl.BlockSpec((d, tf), lambda b, i, f: (0, f)),
                  pl.BlockSpec((d, tf), lambda b, i, f: (0, f)),
                  pl.BlockSpec((tf, d), lambda b, i, f: (f, 0)),
                  rows(),
                  pl.BlockSpec((1, 1, d), lambda b, i, f: (b, 0, 0)),
                  pl.BlockSpec((1, d), lambda b, i, f: (0, 0))],
        out_specs=rows(),
        out_shape=jax.ShapeDtypeStruct((bsz, s, d), F32),
        scratch_shapes=[pltpu.VMEM((tm, d), F32)],
        compiler_params=pltpu.CompilerParams(
            dimension_semantics=("arbitrary", "arbitrary", "arbitrary"),
            vmem_limit_bytes=_vmem_limit(vmem)),
        name="ffn",
    )(h2, wg, wu, wd, x1, gate, final_g.reshape(1, d))


def kernel(x, c, w_ada, b_ada, norm1_g, w_in, lam_q1, lam_k1, lam_q2, lam_k2, diff_subln_g,
           ret_gn_w, ret_gn_b, w_out, norm2_g, w_ffn_gate, w_ffn_up, w_ffn_down, final_g):
    bsz, s, d = x.shape
    depth = w_ada.shape[0]
    assert depth == 1, "the final RMSNorm is fused into the FFN kernel of the only layer"
    n_heads = d // (2 * HEAD_DIM)
    diff_qk_w = n_heads * 2 * HEAD_DIM
    ret_col0 = 2 * diff_qk_w + d
    slopes = jnp.asarray([2.0 ** (-8.0 * (h + 1) / n_heads) for h in range(n_heads)], F32)
    log_g = jnp.asarray([math.log(1.0 - 2.0 ** (-5 - h)) for h in range(n_heads)], F32)

    for l in range(depth):
        lam_init = 0.8 - 0.6 * math.exp(-0.3 * l)
        mod = _adaln(c, w_ada[l], b_ada[l])
        shift1, scale1, gate1, shift2, scale2, gate2 = (
            m.reshape(bsz, 1, d) for m in jnp.split(mod, 6, axis=-1))
        proj = _inproj(x, norm1_g[l], scale1, shift1, w_in[l])
        ya = _diff_attention(proj, slopes, lam_q1[l:l + 1], lam_k1[l:l + 1], lam_q2[l:l + 1],
                             lam_k2[l:l + 1], diff_subln_g[l], n_heads=n_heads, d_model=d,
                             lam_init=lam_init)
        merged = _retention_merge(proj, ya, log_g, ret_gn_w[l], ret_gn_b[l],
                                  n_heads=n_heads, d_model=d, col0=ret_col0)
        x1, h2 = _outproj(merged, w_out[l].astype(BF16), x, gate1, norm2_g[l], scale2, shift2)
        x = _ffn(h2, w_ffn_gate[l].astype(BF16), w_ffn_up[l].astype(BF16),
                 w_ffn_down[l].astype(BF16), x1, gate2, final_g)
    return x
```

```python
import functools
import math

import jax
import jax.numpy as jnp
from jax import lax
from jax.experimental import pallas as pl
from jax.experimental.pallas import tpu as pltpu

F32 = jnp.float32
BF16 = jnp.bfloat16

HEAD_DIM = 128
NORM_EPS = 1e-6
LOG2E = math.log2(math.e)

V7X_VMEM_BYTES = 64 * 1024 * 1024
V7X_SUBLANES = 8

ADA_TN = 1024
INPROJ_TM = 1024
INPROJ_TN = 1024
ATTN_BQ = 512
ATTN_BK = 512
ATTN_STRIP = 256
RET_BLOCK = 512
RET_HEADS = 4
OUTPROJ_TM = 512
FFN_TM = 512
FFN_TF = 512
FFN_STRIP = 256


def _vmem_limit(nbytes):
    return int(min(nbytes + 16 * 1024 * 1024, V7X_VMEM_BYTES - 4 * 1024 * 1024))


def _sigmoid(x):
    return 1.0 / (1.0 + jnp.exp(-x))


def _adaln_kernel(c_ref, w_ref, b_ref, o_ref):
    c = c_ref[...]
    sc = c * _sigmoid(c)
    o_ref[...] = jnp.dot(sc, w_ref[...], preferred_element_type=F32,
                         precision=lax.Precision.HIGHEST) + b_ref[...]


def _adaln(c, w, b):
    bsz, d = c.shape
    n = w.shape[1]
    rows = -(-bsz // V7X_SUBLANES) * V7X_SUBLANES
    c_pad = jnp.pad(c, ((0, rows - bsz), (0, 0)))
    out = pl.pallas_call(
        _adaln_kernel,
        grid=(n // ADA_TN,),
        in_specs=[pl.BlockSpec((rows, d), lambda j: (0, 0)),
                  pl.BlockSpec((d, ADA_TN), lambda j: (0, j)),
                  pl.BlockSpec((1, ADA_TN), lambda j: (0, j))],
        out_specs=pl.BlockSpec((rows, ADA_TN), lambda j: (0, j)),
        out_shape=jax.ShapeDtypeStruct((rows, n), F32),
        compiler_params=pltpu.CompilerParams(
            dimension_semantics=("arbitrary",),
            vmem_limit_bytes=_vmem_limit(2 * d * ADA_TN * 4)),
        name="adaln_mod",
    )(c_pad, w, b.reshape(1, n))
    return out[:bsz]


def _modulated_rmsnorm(x, g, scale, shift):
    ms = jnp.mean(x * x, axis=-1, keepdims=True)
    y = x * lax.rsqrt(ms + NORM_EPS) * g
    return y * (1.0 + scale) + shift


def _inproj_kernel(x_ref, g_ref, scale_ref, shift_ref, w_ref, o_ref, h_ref):
    @pl.when(pl.program_id(2) == 0)
    def _():
        h = _modulated_rmsnorm(x_ref[0], g_ref[...], scale_ref[0], shift_ref[0])
        h_ref[...] = h.astype(BF16)

    w = w_ref[...].astype(BF16)
    o_ref[0] = jnp.dot(h_ref[...], w, preferred_element_type=F32).astype(o_ref.dtype)


def _inproj(x, g, scale, shift, w):
    bsz, s, d = x.shape
    n = w.shape[1]
    tm, tn = INPROJ_TM, INPROJ_TN
    vmem = 2 * tm * d * 4 + 2 * d * tn * 4 + d * tn * 2 + 2 * tm * tn * 2 + tm * d * 2
    return pl.pallas_call(
        _inproj_kernel,
        grid=(bsz, s // tm, n // tn),
        in_specs=[pl.BlockSpec((1, tm, d), lambda b, i, j: (b, i, 0)),
                  pl.BlockSpec((1, d), lambda b, i, j: (0, 0)),
                  pl.BlockSpec((1, 1, d), lambda b, i, j: (b, 0, 0)),
                  pl.BlockSpec((1, 1, d), lambda b, i, j: (b, 0, 0)),
                  pl.BlockSpec((d, tn), lambda b, i, j: (0, j))],
        out_specs=pl.BlockSpec((1, tm, tn), lambda b, i, j: (b, i, j)),
        out_shape=jax.ShapeDtypeStruct((bsz, s, n), BF16),
        scratch_shapes=[pltpu.VMEM((tm, d), BF16)],
        compiler_params=pltpu.CompilerParams(
            dimension_semantics=("arbitrary", "arbitrary", "arbitrary"),
            vmem_limit_bytes=_vmem_limit(vmem)),
        name="inproj",
    )(x, g.reshape(1, d), scale, shift, w)


def _diff_attn_kernel(slopes_ref, q1_ref, q2_ref, k1_ref, k2_ref, v_ref,
                      lq1_ref, lk1_ref, lq2_ref, lk2_ref, g_ref, o_ref,
                      bias_ref, vt_ref, acc1_ref, acc2_ref, p1_ref, p2_ref, t2_ref, *, lam_init, bq, bk):
    h = pl.program_id(1)
    i = pl.program_id(2)
    slope_l2 = slopes_ref[h] * LOG2E
    n_kv = v_ref.shape[1] // bk
    ratio = bq // bk

    @pl.when(i == 0)
    def _():
        key = lax.broadcasted_iota(jnp.int32, (bk, bq), 0)
        qry = lax.broadcasted_iota(jnp.int32, (bk, bq), 1)
        bias = (key - qry).astype(F32) * slope_l2
        bias_ref[0] = bias
        for d in range(ratio):
            bias_ref[1 + d] = jnp.where(qry >= key + d * bk, bias, -jnp.inf)

        def xpose(c, _):
            start = pl.multiple_of(c * bk, bk)
            vt_ref[:, pl.ds(start, bk)] = v_ref[0, pl.ds(start, bk), :].T
            return 0
        lax.fori_loop(0, n_kv, xpose, 0)

    qscale = (HEAD_DIM ** -0.5) * LOG2E
    q1t = (q1_ref[0].astype(F32) * qscale).astype(BF16).T
    q2t = (q2_ref[0].astype(F32) * qscale).astype(BF16).T

    acc1_ref[...] = jnp.zeros_like(acc1_ref)
    acc2_ref[...] = jnp.zeros_like(acc2_ref)
    strips = [slice(c, c + ATTN_STRIP) for c in range(0, bq, ATTN_STRIP)]
    cat = lambda xs: jnp.concatenate(xs, axis=1)

    def scores(qt, k_ref, start):
        k = k_ref[0, pl.ds(start, bk), :]
        return [jnp.dot(k, qt[:, cs], preferred_element_type=F32) for cs in strips]

    def tile_offset(j):
        return slope_l2 * (i * bq - j * bk).astype(F32)

    def bias_variant(j):
        d = j - ratio * i
        return jnp.where(d >= 0, d + 1, 0)

    def softmax(ts, mxs, off, m_old, l_old, p_ref):
        m_out, l_out, a_out = [], [], []
        for cs, t, mx in zip(strips, ts, mxs):
            m_new = jnp.maximum(m_old[:, cs], mx)
            p = jnp.exp2(t() - (m_new + off))
            alpha = jnp.exp2(m_old[:, cs] - m_new)
            l_out.append(alpha * l_old[:, cs] + jnp.sum(p, axis=0, keepdims=True))
            m_out.append(m_new)
            a_out.append(alpha)
            p_ref[:, cs] = p.astype(BF16)
        return cat(m_out), cat(l_out), cat(a_out)

    def softmax_now(ss, sel, off, m_old, l_old, p_ref):
        ts = [s + bias_ref[sel, :, cs] for cs, s in zip(strips, ss)]
        mxs = [jnp.max(t, axis=0, keepdims=True) - off for t in ts]
        return softmax([lambda t=t: t for t in ts], mxs, off, m_old, l_old, p_ref)

    def park_scores(ss, sel, off):
        mxs = []
        for cs, s in zip(strips, ss):
            t = s + bias_ref[sel, :, cs]
            t2_ref[:, cs] = t
            mxs.append(jnp.max(t, axis=0, keepdims=True) - off)
        return cat(mxs)

    def softmax_parked(mx, off, m_old, l_old, p_ref):
        ts = [lambda cs=cs: t2_ref[:, cs] for cs in strips]
        return softmax(ts, [mx[:, cs] for cs in strips], off, m_old, l_old, p_ref)

    def pv(vt, acc_ref, p_ref, alpha):
        acc_ref[...] = alpha * acc_ref[...] + jnp.dot(vt, p_ref[...], preferred_element_type=F32)

    neg = jnp.full((1, bq), -jnp.inf, F32)
    zero = jnp.zeros((1, bq), F32)
    last = ratio * i + ratio - 1

    sel0 = bias_variant(0)
    off0 = tile_offset(0)
    s1 = scores(q1t, k1_ref, 0)
    s2 = scores(q2t, k2_ref, 0)
    m1, l1, a1 = softmax_now(s1, sel0, off0, neg, zero, p1_ref)
    mx2 = park_scores(s2, sel0, off0)

    def step(j, carry):
        m1, l1, a1, m2, l2, mx2 = carry
        sel = bias_variant(j)
        start = pl.multiple_of(j * bk, bk)
        vt_prev = vt_ref[:, pl.ds(pl.multiple_of((j - 1) * bk, bk), bk)]
        off = tile_offset(j)
        s1 = scores(q1t, k1_ref, start)
        pv(vt_prev, acc1_ref, p1_ref, a1)
        m2, l2, a2 = softmax_parked(mx2, tile_offset(j - 1), m2, l2, p2_ref)
        s2 = scores(q2t, k2_ref, start)
        m1, l1, a1 = softmax_now(s1, sel, off, m1, l1, p1_ref)
        pv(vt_prev, acc2_ref, p2_ref, a2)
        mx2 = park_scores(s2, sel, off)
        return m1, l1, a1, m2, l2, mx2

    m1, l1, a1, m2, l2, mx2 = lax.fori_loop(1, last + 1, step, (m1, l1, a1, neg, zero, mx2))

    vt_last = vt_ref[:, pl.ds(pl.multiple_of(last * bk, bk), bk)]
    pv(vt_last, acc1_ref, p1_ref, a1)
    m2, l2, a2 = softmax_parked(mx2, tile_offset(last), m2, l2, p2_ref)
    pv(vt_last, acc2_ref, p2_ref, a2)

    lam = (jnp.exp(jnp.sum(lq1_ref[...] * lk1_ref[...], axis=-1, keepdims=True))
           - jnp.exp(jnp.sum(lq2_ref[...] * lk2_ref[...], axis=-1, keepdims=True))
           + lam_init)
    yt = acc1_ref[...] * (1.0 / l1) - acc2_ref[...] * (lam / l2)
    ynt = yt * lax.rsqrt(jnp.mean(yt * yt, axis=0, keepdims=True) + NORM_EPS)
    o_ref[0] = (ynt * (g_ref[...] * (1.0 - lam_init))).astype(o_ref.dtype).T


def _diff_attention(proj, slopes, lq1, lk1, lq2, lk2, subln_g, *, n_heads, d_model, lam_init):
    bsz, s, _ = proj.shape
    bq, bk = ATTN_BQ, ATTN_BK
    dv = d_model // n_heads
    k_col0 = (n_heads * 2 * HEAD_DIM) // HEAD_DIM
    v_col0 = (2 * n_heads * 2 * HEAD_DIM) // dv
    qspec = lambda m: pl.BlockSpec((1, bq, HEAD_DIM), lambda b, h, i, sl: (b, i, 2 * h + m))
    kspec = lambda m: pl.BlockSpec((1, s, HEAD_DIM), lambda b, h, i, sl: (b, 0, k_col0 + 2 * h + m))
    vec = pl.BlockSpec((1, HEAD_DIM), lambda b, h, i, sl: (0, 0))
    n_bias = 1 + bq // bk
    vmem = (2 * 2 * s * HEAD_DIM * 2 + 3 * s * dv * 2 + (n_bias + 1) * bk * bq * 4 + 2 * bq * dv * 4
            + 2 * bk * bq * 2 + 6 * bk * bq * 4)
    kernel = functools.partial(_diff_attn_kernel, lam_init=lam_init, bq=bq, bk=bk)
    return pl.pallas_call(
        kernel,
        grid_spec=pltpu.PrefetchScalarGridSpec(
            num_scalar_prefetch=1,
            grid=(bsz, n_heads, s // bq),
            in_specs=[qspec(0), qspec(1), kspec(0), kspec(1),
                      pl.BlockSpec((1, s, dv), lambda b, h, i, sl: (b, 0, v_col0 + h)),
                      vec, vec, vec, vec,
                      pl.BlockSpec((dv, 1), lambda b, h, i, sl: (h, 0))],
            out_specs=pl.BlockSpec((1, bq, dv), lambda b, h, i, sl: (b, i, h)),
            scratch_shapes=[pltpu.VMEM((n_bias, bk, bq), F32),
                            pltpu.VMEM((dv, s), BF16),
                            pltpu.VMEM((dv, bq), F32), pltpu.VMEM((dv, bq), F32),
                            pltpu.VMEM((bk, bq), BF16), pltpu.VMEM((bk, bq), BF16),
                            pltpu.VMEM((bk, bq), F32)]),
        out_shape=jax.ShapeDtypeStruct((bsz, s, d_model), BF16),
        compiler_params=pltpu.CompilerParams(
            dimension_semantics=("arbitrary", "arbitrary", "arbitrary"),
            vmem_limit_bytes=_vmem_limit(vmem)),
        name="diff_attention",
    )(slopes, proj, proj, proj, proj, proj, lq1, lk1, lq2, lk2, subln_g.reshape(d_model, 1))


def _retention_merge_kernel(logg_ref, q_ref, k_ref, v_ref, rg_ref, ga_ref, gb_ref, ya_ref,
                            gnw_ref, gnb_ref, o_ref, decay_ref, state_ref, xi_ref, zeta_ref, *, blk, dv):
    hp = pl.program_id(1)
    n = pl.program_id(2)
    kscale = HEAD_DIM ** -0.5
    heads = range(RET_HEADS)
    log_gs = [logg_ref[hp * RET_HEADS + hh] for hh in heads]

    @pl.when(n == 0)
    def _():
        row = lax.broadcasted_iota(jnp.int32, (blk, blk), 0)
        col = lax.broadcasted_iota(jnp.int32, (blk, blk), 1)
        rel = (row - col).astype(F32)
        idx = lax.broadcasted_iota(jnp.int32, (blk, 1), 0).astype(F32)
        for hh in heads:
            decay_ref[hh] = jnp.where(rel >= 0, jnp.exp(log_gs[hh] * jnp.maximum(rel, 0.0)) * kscale, 0.0)
            xi_ref[hh] = jnp.exp(log_gs[hh] * (idx + 1.0))
            zeta_ref[hh] = jnp.exp(log_gs[hh] * (blk - 1.0 - idx)) * kscale
        state_ref[...] = jnp.zeros_like(state_ref)

    def mix(hh):
        log_g = log_gs[hh]
        qk = slice(hh * HEAD_DIM, (hh + 1) * HEAD_DIM)
        vv = slice(hh * dv, (hh + 1) * dv)
        q = q_ref[0, :, qk]
        k = k_ref[0, :, qk]
        v = v_ref[0, :, vv]
        s = lax.dot_general(q, k, (((1,), (1,)), ((), ())), preferred_element_type=F32)
        intra = jnp.dot((s * decay_ref[hh]).astype(BF16), v, preferred_element_type=F32)
        state = state_ref[hh]
        cross = jnp.dot((q.astype(F32) * xi_ref[hh]).astype(BF16), state.astype(BF16),
                        preferred_element_type=F32)
        kz = (k.astype(F32) * zeta_ref[hh]).astype(BF16)
        kv = lax.dot_general(kz, v, (((0,), (0,)), ((), ())), preferred_element_type=F32)
        state_ref[hh] = jnp.exp(log_g * blk) * state + kv
        return intra + cross

    def finish(hh, y):
        vv = slice(hh * dv, (hh + 1) * dv)
        mu = jnp.mean(y, axis=-1, keepdims=True)
        yc = y - mu
        var = jnp.mean(yc * yc, axis=-1, keepdims=True)
        gn = yc * lax.rsqrt(var + NORM_EPS) * gnw_ref[:, vv] + gnb_ref[:, vv]
        rg = rg_ref[0, :, vv].astype(F32)
        yb = rg * _sigmoid(rg) * gn
        merged = (_sigmoid(ga_ref[0, :, vv].astype(F32)) * ya_ref[0, :, vv].astype(F32)
                  + _sigmoid(gb_ref[0, :, vv].astype(F32)) * yb)
        o_ref[0, :, vv] = merged.astype(o_ref.dtype)

    ys = [mix(hh) for hh in heads]
    for hh in heads:
        finish(hh, ys[hh])


def _retention_merge(proj, ya, log_g, gn_w, gn_b, *, n_heads, d_model, col0):
    bsz, s, _ = proj.shape
    blk = RET_BLOCK
    dv = d_model // n_heads
    nw = RET_HEADS * HEAD_DIM
    ww = RET_HEADS * dv
    groups = n_heads // RET_HEADS
    q0 = col0 // nw
    k0 = q0 + groups
    v0 = (col0 + 2 * n_heads * HEAD_DIM) // ww
    rg0 = v0 + groups
    ga0 = rg0 + groups
    gb0 = ga0 + groups
    wide = lambda c0: pl.BlockSpec((1, blk, ww), lambda b, h, n, lg: (b, n, c0 + h))
    narrow = lambda c0: pl.BlockSpec((1, blk, nw), lambda b, h, n, lg: (b, n, c0 + h))
    chan = pl.BlockSpec((1, ww), lambda b, h, n, lg: (0, h))
    vmem = 2 * (2 * blk * nw * 2 + 6 * blk * ww * 2) + RET_HEADS * (blk * blk * 4 + HEAD_DIM * dv * 4) \
        + 4 * RET_HEADS * blk * blk * 4
    kernel = functools.partial(_retention_merge_kernel, blk=blk, dv=dv)
    return pl.pallas_call(
        kernel,
        grid_spec=pltpu.PrefetchScalarGridSpec(
            num_scalar_prefetch=1,
            grid=(bsz, groups, s // blk),
            in_specs=[narrow(q0), narrow(k0), wide(v0), wide(rg0), wide(ga0), wide(gb0),
                      wide(0), chan, chan],
            out_specs=wide(0),
            scratch_shapes=[pltpu.VMEM((RET_HEADS, blk, blk), F32),
                            pltpu.VMEM((RET_HEADS, HEAD_DIM, dv), F32),
                            pltpu.VMEM((RET_HEADS, blk, 1), F32), pltpu.VMEM((RET_HEADS, blk, 1), F32)]),
        out_shape=jax.ShapeDtypeStruct((bsz, s, d_model), BF16),
        compiler_params=pltpu.CompilerParams(
            dimension_semantics=("arbitrary", "arbitrary", "arbitrary"),
            vmem_limit_bytes=_vmem_limit(vmem)),
        name="retention_merge",
    )(log_g, proj, proj, proj, proj, proj, proj, ya, gn_w.reshape(1, d_model), gn_b.reshape(1, d_model))


def _outproj_kernel(m_ref, w_ref, x_ref, gate_ref, g_ref, scale_ref, shift_ref, x1_ref, h2_ref):
    half = m_ref.shape[1] // 2
    rows = [slice(0, half), slice(half, 2 * half)]
    ys = [jnp.dot(m_ref[0, r, :], w_ref[...], preferred_element_type=F32) for r in rows]
    for r, y in zip(rows, ys):
        x1 = x_ref[0, r, :] + gate_ref[0] * y
        x1_ref[0, r, :] = x1
        h2_ref[0, r, :] = _modulated_rmsnorm(x1, g_ref[...], scale_ref[0],
                                             shift_ref[0]).astype(h2_ref.dtype)


def _outproj(merged, w_bf16, x, gate, g, scale, shift):
    bsz, s, d = x.shape
    tm = OUTPROJ_TM
    rows = pl.BlockSpec((1, tm, d), lambda b, i: (b, i, 0))
    mod = pl.BlockSpec((1, 1, d), lambda b, i: (b, 0, 0))
    vmem = 2 * (tm * d * 2 + d * d * 2 + tm * d * 4 + tm * d * 4 + tm * d * 2)
    return pl.pallas_call(
        _outproj_kernel,
        grid=(bsz, s // tm),
        in_specs=[rows, pl.BlockSpec((d, d), lambda b, i: (0, 0)), rows, mod,
                  pl.BlockSpec((1, d), lambda b, i: (0, 0)), mod, mod],
        out_specs=[rows, rows],
        out_shape=[jax.ShapeDtypeStruct((bsz, s, d), F32), jax.ShapeDtypeStruct((bsz, s, d), BF16)],
        compiler_params=pltpu.CompilerParams(
            dimension_semantics=("arbitrary", "arbitrary"),
            vmem_limit_bytes=_vmem_limit(vmem)),
        name="outproj",
    )(merged, w_bf16, x, gate, g.reshape(1, d), scale, shift)


def _ffn_kernel(h_ref, wg_ref, wu_ref, wd_ref, x1_ref, gate_ref, fg_ref, o_ref, acc_ref):
    f = pl.program_id(2)

    @pl.when(f == 0)
    def _():
        acc_ref[...] = jnp.zeros_like(acc_ref)

    h = h_ref[0]
    parts = []
    for c in range(0, wg_ref.shape[1], FFN_STRIP):
        cs = slice(c, c + FFN_STRIP)
        g = jnp.dot(h, wg_ref[:, cs], preferred_element_type=F32)
        u = jnp.dot(h, wu_ref[:, cs], preferred_element_type=F32)
        parts.append((g * _sigmoid(g) * u).astype(BF16))
    a = jnp.concatenate(parts, axis=1)
    acc_ref[...] += jnp.dot(a, wd_ref[...], preferred_element_type=F32)

    @pl.when(f == pl.num_programs(2) - 1)
    def _():
        x2 = x1_ref[0] + gate_ref[0] * acc_ref[...]
        ms = jnp.mean(x2 * x2, axis=-1, keepdims=True)
        o_ref[0] = x2 * lax.rsqrt(ms + NORM_EPS) * fg_ref[...]


def _ffn(h2, wg, wu, wd, x1, gate, final_g):
    bsz, s, d = x1.shape
    dff = wg.shape[1]
    tm, tf = FFN_TM, FFN_TF
    rows = lambda: pl.BlockSpec((1, tm, d), lambda b, i, f: (b, i, 0))
    vmem = 2 * (tm * d * 2 + 2 * d * tf * 2 + tf * d * 2 + tm * d * 4 + tm * d * 4) + tm * d * 4
    return pl.pallas_call(
        _ffn_kernel,
        grid=(bsz, s // tm, dff // tf),
        in_specs=[rows(),
                  pl.BlockSpec((d, tf), lambda b, i, f: (0, f)),
                  pl.BlockSpec((d, tf), lambda b, i, f: (0, f)),
                  pl.BlockSpec((tf, d), lambda b, i, f: (f, 0)),
                  rows(),
                  pl.BlockSpec((1, 1, d), lambda b, i, f: (b, 0, 0)),
                  pl.BlockSpec((1, d), lambda b, i, f: (0, 0))],
        out_specs=rows(),
        out_shape=jax.ShapeDtypeStruct((bsz, s, d), F32),
        scratch_shapes=[pltpu.VMEM((tm, d), F32)],
        compiler_params=pltpu.CompilerParams(
            dimension_semantics=("arbitrary", "arbitrary", "arbitrary"),
            vmem_limit_bytes=_vmem_limit(vmem)),
        name="ffn",
    )(h2, wg, wu, wd, x1, gate, final_g.reshape(1, d))


def kernel(x, c, w_ada, b_ada, norm1_g, w_in, lam_q1, lam_k1, lam_q2, lam_k2, diff_subln_g,
           ret_gn_w, ret_gn_b, w_out, norm2_g, w_ffn_gate, w_ffn_up, w_ffn_down, final_g):
    bsz, s, d = x.shape
    depth = w_ada.shape[0]
    assert depth == 1, "the final RMSNorm is fused into the FFN kernel of the only layer"
    n_heads = d // (2 * HEAD_DIM)
    diff_qk_w = n_heads * 2 * HEAD_DIM
    ret_col0 = 2 * diff_qk_w + d
    slopes = jnp.asarray([2.0 ** (-8.0 * (h + 1) / n_heads) for h in range(n_heads)], F32)
    log_g = jnp.asarray([math.log(1.0 - 2.0 ** (-5 - h)) for h in range(n_heads)], F32)

    for l in range(depth):
        lam_init = 0.8 - 0.6 * math.exp(-0.3 * l)
        mod = _adaln(c, w_ada[l], b_ada[l])
        shift1, scale1, gate1, shift2, scale2, gate2 = (
            m.reshape(bsz, 1, d) for m in jnp.split(mod, 6, axis=-1))
        proj = _inproj(x, norm1_g[l], scale1, shift1, w_in[l])
        ya = _diff_attention(proj, slopes, lam_q1[l:l + 1], lam_k1[l:l + 1], lam_q2[l:l + 1],
                             lam_k2[l:l + 1], diff_subln_g[l], n_heads=n_heads, d_model=d,
                             lam_init=lam_init)
        merged = _retention_merge(proj, ya, log_g, ret_gn_w[l], ret_gn_b[l],
                                  n_heads=n_heads, d_model=d, col0=ret_col0)
        x1, h2 = _outproj(merged, w_out[l].astype(BF16), x, gate1, norm2_g[l], scale2, shift2)
        x = _ffn(h2, w_ffn_gate[l].astype(BF16), w_ffn_up[l].astype(BF16),
                 w_ffn_down[l].astype(BF16), x1, gate2, final_g)
    return x
```

```python
import functools
import math

import jax
import jax.numpy as jnp
from jax import lax
from jax.experimental import pallas as pl
from jax.experimental.pallas import tpu as pltpu

F32 = jnp.float32
BF16 = jnp.bfloat16

HEAD_DIM = 128
NORM_EPS = 1e-6
LOG2E = math.log2(math.e)

V7X_VMEM_BYTES = 64 * 1024 * 1024
V7X_SUBLANES = 8

ADA_TN = 1024
INPROJ_TM = 1024
INPROJ_TN = 1024
ATTN_BQ = 512
ATTN_BK = 512
ATTN_STRIP = 256
ATTN_SOFTMAX_STRIP = 128
RET_BLOCK = 512
RET_HEADS = 4
OUTPROJ_TM = 512
FFN_TM = 512
FFN_TF = 512
FFN_STRIP = 256


def _vmem_limit(nbytes):
    return int(min(nbytes + 16 * 1024 * 1024, V7X_VMEM_BYTES - 4 * 1024 * 1024))


def _sigmoid(x):
    return 1.0 / (1.0 + jnp.exp(-x))


def _adaln_kernel(c_ref, w_ref, b_ref, o_ref):
    c = c_ref[...]
    sc = c * _sigmoid(c)
    o_ref[...] = jnp.dot(sc, w_ref[...], preferred_element_type=F32,
                         precision=lax.Precision.HIGHEST) + b_ref[...]


def _adaln(c, w, b):
    bsz, d = c.shape
    n = w.shape[1]
    rows = -(-bsz // V7X_SUBLANES) * V7X_SUBLANES
    c_pad = jnp.pad(c, ((0, rows - bsz), (0, 0)))
    out = pl.pallas_call(
        _adaln_kernel,
        grid=(n // ADA_TN,),
        in_specs=[pl.BlockSpec((rows, d), lambda j: (0, 0)),
                  pl.BlockSpec((d, ADA_TN), lambda j: (0, j)),
                  pl.BlockSpec((1, ADA_TN), lambda j: (0, j))],
        out_specs=pl.BlockSpec((rows, ADA_TN), lambda j: (0, j)),
        out_shape=jax.ShapeDtypeStruct((rows, n), F32),
        compiler_params=pltpu.CompilerParams(
            dimension_semantics=("arbitrary",),
            vmem_limit_bytes=_vmem_limit(2 * d * ADA_TN * 4)),
        name="adaln_mod",
    )(c_pad, w, b.reshape(1, n))
    return out[:bsz]


def _modulated_rmsnorm(x, g, scale, shift):
    ms = jnp.mean(x * x, axis=-1, keepdims=True)
    y = x * lax.rsqrt(ms + NORM_EPS) * g
    return y * (1.0 + scale) + shift


def _inproj_kernel(x_ref, g_ref, scale_ref, shift_ref, w_ref, o_ref, h_ref):
    @pl.when(pl.program_id(2) == 0)
    def _():
        h = _modulated_rmsnorm(x_ref[0], g_ref[...], scale_ref[0], shift_ref[0])
        h_ref[...] = h.astype(BF16)

    w = w_ref[...].astype(BF16)
    o_ref[0] = jnp.dot(h_ref[...], w, preferred_element_type=F32).astype(o_ref.dtype)


def _inproj(x, g, scale, shift, w):
    bsz, s, d = x.shape
    n = w.shape[1]
    tm, tn = INPROJ_TM, INPROJ_TN
    vmem = 2 * tm * d * 4 + 2 * d * tn * 4 + d * tn * 2 + 2 * tm * tn * 2 + tm * d * 2
    return pl.pallas_call(
        _inproj_kernel,
        grid=(bsz, s // tm, n // tn),
        in_specs=[pl.BlockSpec((1, tm, d), lambda b, i, j: (b, i, 0)),
                  pl.BlockSpec((1, d), lambda b, i, j: (0, 0)),
                  pl.BlockSpec((1, 1, d), lambda b, i, j: (b, 0, 0)),
                  pl.BlockSpec((1, 1, d), lambda b, i, j: (b, 0, 0)),
                  pl.BlockSpec((d, tn), lambda b, i, j: (0, j))],
        out_specs=pl.BlockSpec((1, tm, tn), lambda b, i, j: (b, i, j)),
        out_shape=jax.ShapeDtypeStruct((bsz, s, n), BF16),
        scratch_shapes=[pltpu.VMEM((tm, d), BF16)],
        compiler_params=pltpu.CompilerParams(
            dimension_semantics=("arbitrary", "arbitrary", "arbitrary"),
            vmem_limit_bytes=_vmem_limit(vmem)),
        name="inproj",
    )(x, g.reshape(1, d), scale, shift, w)


def _diff_attn_kernel(slopes_ref, q1_ref, q2_ref, k1_ref, k2_ref, v_ref,
                      lq1_ref, lk1_ref, lq2_ref, lk2_ref, g_ref, o_ref,
                      bias_ref, vt_ref, acc1_ref, acc2_ref, p1_ref, p2_ref, t2_ref, *, lam_init, bq, bk):
    h = pl.program_id(1)
    i = pl.program_id(2)
    slope_l2 = slopes_ref[h] * LOG2E
    n_kv = v_ref.shape[1] // bk
    ratio = bq // bk

    @pl.when(i == 0)
    def _():
        key = lax.broadcasted_iota(jnp.int32, (bk, bq), 0)
        qry = lax.broadcasted_iota(jnp.int32, (bk, bq), 1)
        bias = (key - qry).astype(F32) * slope_l2
        bias_ref[0] = bias
        for d in range(ratio):
            bias_ref[1 + d] = jnp.where(qry >= key + d * bk, bias, -jnp.inf)

        def xpose(c, _):
            start = pl.multiple_of(c * bk, bk)
            vt_ref[:, pl.ds(start, bk)] = v_ref[0, pl.ds(start, bk), :].T
            return 0
        lax.fori_loop(0, n_kv, xpose, 0)

    qscale = (HEAD_DIM ** -0.5) * LOG2E
    q1t = (q1_ref[0].astype(F32) * qscale).astype(BF16).T
    q2t = (q2_ref[0].astype(F32) * qscale).astype(BF16).T

    acc1_ref[...] = jnp.zeros_like(acc1_ref)
    acc2_ref[...] = jnp.zeros_like(acc2_ref)
    mm_strips = [slice(c, c + ATTN_STRIP) for c in range(0, bq, ATTN_STRIP)]
    strips = [slice(c, c + ATTN_SOFTMAX_STRIP) for c in range(0, bq, ATTN_SOFTMAX_STRIP)]
    cat = lambda xs: jnp.concatenate(xs, axis=1)

    def scores(qt, k_ref, start):
        k = k_ref[0, pl.ds(start, bk), :]
        out = []
        for cs in mm_strips:
            s = jnp.dot(k, qt[:, cs], preferred_element_type=F32)
            out += [s[:, c:c + ATTN_SOFTMAX_STRIP] for c in range(0, ATTN_STRIP, ATTN_SOFTMAX_STRIP)]
        return out

    def tile_offset(j):
        return slope_l2 * (i * bq - j * bk).astype(F32)

    def bias_variant(j):
        d = j - ratio * i
        return jnp.where(d >= 0, d + 1, 0)

    def softmax(ts, mxs, off, m_old, l_old, p_ref):
        m_out, l_out, a_out = [], [], []
        for cs, t, mx in zip(strips, ts, mxs):
            m_new = jnp.maximum(m_old[:, cs], mx)
            p = jnp.exp2(t() - (m_new + off))
            alpha = jnp.exp2(m_old[:, cs] - m_new)
            l_out.append(alpha * l_old[:, cs] + jnp.sum(p, axis=0, keepdims=True))
            m_out.append(m_new)
            a_out.append(alpha)
            p_ref[:, cs] = p.astype(BF16)
        return cat(m_out), cat(l_out), cat(a_out)

    def softmax_now(ss, sel, off, m_old, l_old, p_ref):
        ts = [s + bias_ref[sel, :, cs] for cs, s in zip(strips, ss)]
        mxs = [jnp.max(t, axis=0, keepdims=True) - off for t in ts]
        return softmax([lambda t=t: t for t in ts], mxs, off, m_old, l_old, p_ref)

    def park_scores(ss, sel, off):
        mxs = []
        for cs, s in zip(strips, ss):
            t = s + bias_ref[sel, :, cs]
            t2_ref[:, cs] = t
            mxs.append(jnp.max(t, axis=0, keepdims=True) - off)
        return cat(mxs)

    def softmax_parked(mx, off, m_old, l_old, p_ref):
        ts = [lambda cs=cs: t2_ref[:, cs] for cs in strips]
        return softmax(ts, [mx[:, cs] for cs in strips], off, m_old, l_old, p_ref)

    def pv(vt, acc_ref, p_ref, alpha):
        acc_ref[...] = alpha * acc_ref[...] + jnp.dot(vt, p_ref[...], preferred_element_type=F32)

    neg = jnp.full((1, bq), -jnp.inf, F32)
    zero = jnp.zeros((1, bq), F32)
    last = ratio * i + ratio - 1

    sel0 = bias_variant(0)
    off0 = tile_offset(0)
    s1 = scores(q1t, k1_ref, 0)
    s2 = scores(q2t, k2_ref, 0)
    m1, l1, a1 = softmax_now(s1, sel0, off0, neg, zero, p1_ref)
    mx2 = park_scores(s2, sel0, off0)

    def step(j, carry):
        m1, l1, a1, m2, l2, mx2 = carry
        sel = bias_variant(j)
        start = pl.multiple_of(j * bk, bk)
        vt_prev = vt_ref[:, pl.ds(pl.multiple_of((j - 1) * bk, bk), bk)]
        off = tile_offset(j)
        s1 = scores(q1t, k1_ref, start)
        pv(vt_prev, acc1_ref, p1_ref, a1)
        m2, l2, a2 = softmax_parked(mx2, tile_offset(j - 1), m2, l2, p2_ref)
        s2 = scores(q2t, k2_ref, start)
        m1, l1, a1 = softmax_now(s1, sel, off, m1, l1, p1_ref)
        pv(vt_prev, acc2_ref, p2_ref, a2)
        mx2 = park_scores(s2, sel, off)
        return m1, l1, a1, m2, l2, mx2

    m1, l1, a1, m2, l2, mx2 = lax.fori_loop(1, last + 1, step, (m1, l1, a1, neg, zero, mx2))

    vt_last = vt_ref[:, pl.ds(pl.multiple_of(last * bk, bk), bk)]
    pv(vt_last, acc1_ref, p1_ref, a1)
    m2, l2, a2 = softmax_parked(mx2, tile_offset(last), m2, l2, p2_ref)
    pv(vt_last, acc2_ref, p2_ref, a2)

    lam = (jnp.exp(jnp.sum(lq1_ref[...] * lk1_ref[...], axis=-1, keepdims=True))
           - jnp.exp(jnp.sum(lq2_ref[...] * lk2_ref[...], axis=-1, keepdims=True))
           + lam_init)
    yt = acc1_ref[...] * (1.0 / l1) - acc2_ref[...] * (lam / l2)
    ynt = yt * lax.rsqrt(jnp.mean(yt * yt, axis=0, keepdims=True) + NORM_EPS)
    o_ref[0] = (ynt * (g_ref[...] * (1.0 - lam_init))).astype(o_ref.dtype).T


def _diff_attention(proj, slopes, lq1, lk1, lq2, lk2, subln_g, *, n_heads, d_model, lam_init):
    bsz, s, _ = proj.shape
    bq, bk = ATTN_BQ, ATTN_BK
    dv = d_model // n_heads
    k_col0 = (n_heads * 2 * HEAD_DIM) // HEAD_DIM
    v_col0 = (2 * n_heads * 2 * HEAD_DIM) // dv
    qspec = lambda m: pl.BlockSpec((1, bq, HEAD_DIM), lambda b, h, i, sl: (b, i, 2 * h + m))
    kspec = lambda m: pl.BlockSpec((1, s, HEAD_DIM), lambda b, h, i, sl: (b, 0, k_col0 + 2 * h + m))
    vec = pl.BlockSpec((1, HEAD_DIM), lambda b, h, i, sl: (0, 0))
    n_bias = 1 + bq // bk
    vmem = (2 * 2 * s * HEAD_DIM * 2 + 3 * s * dv * 2 + (n_bias + 1) * bk * bq * 4 + 2 * bq * dv * 4
            + 2 * bk * bq * 2 + 6 * bk * bq * 4)
    kernel = functools.partial(_diff_attn_kernel, lam_init=lam_init, bq=bq, bk=bk)
    return pl.pallas_call(
        kernel,
        grid_spec=pltpu.PrefetchScalarGridSpec(
            num_scalar_prefetch=1,
            grid=(bsz, n_heads, s // bq),
            in_specs=[qspec(0), qspec(1), kspec(0), kspec(1),
                      pl.BlockSpec((1, s, dv), lambda b, h, i, sl: (b, 0, v_col0 + h)),
                      vec, vec, vec, vec,
                      pl.BlockSpec((dv, 1), lambda b, h, i, sl: (h, 0))],
            out_specs=pl.BlockSpec((1, bq, dv), lambda b, h, i, sl: (b, i, h)),
            scratch_shapes=[pltpu.VMEM((n_bias, bk, bq), F32),
                            pltpu.VMEM((dv, s), BF16),
                            pltpu.VMEM((dv, bq), F32), pltpu.VMEM((dv, bq), F32),
                            pltpu.VMEM((bk, bq), BF16), pltpu.VMEM((bk, bq), BF16),
                            pltpu.VMEM((bk, bq), F32)]),
        out_shape=jax.ShapeDtypeStruct((bsz, s, d_model), BF16),
        compiler_params=pltpu.CompilerParams(
            dimension_semantics=("arbitrary", "arbitrary", "arbitrary"),
            vmem_limit_bytes=_vmem_limit(vmem)),
        name="diff_attention",
    )(slopes, proj, proj, proj, proj, proj, lq1, lk1, lq2, lk2, subln_g.reshape(d_model, 1))


def _retention_merge_kernel(logg_ref, q_ref, k_ref, v_ref, rg_ref, ga_ref, gb_ref, ya_ref,
                            gnw_ref, gnb_ref, o_ref, decay_ref, state_ref, xi_ref, zeta_ref, *, blk, dv):
    hp = pl.program_id(1)
    n = pl.program_id(2)
    kscale = HEAD_DIM ** -0.5
    heads = range(RET_HEADS)
    log_gs = [logg_ref[hp * RET_HEADS + hh] for hh in heads]

    @pl.when(n == 0)
    def _():
        row = lax.broadcasted_iota(jnp.int32, (blk, blk), 0)
        col = lax.broadcasted_iota(jnp.int32, (blk, blk), 1)
        rel = (row - col).astype(F32)
        idx = lax.broadcasted_iota(jnp.int32, (blk, 1), 0).astype(F32)
        for hh in heads:
            decay_ref[hh] = jnp.where(rel >= 0, jnp.exp(log_gs[hh] * jnp.maximum(rel, 0.0)) * kscale, 0.0)
            xi_ref[hh] = jnp.exp(log_gs[hh] * (idx + 1.0))
            zeta_ref[hh] = jnp.exp(log_gs[hh] * (blk - 1.0 - idx)) * kscale
        state_ref[...] = jnp.zeros_like(state_ref)

    def mix(hh):
        log_g = log_gs[hh]
        qk = slice(hh * HEAD_DIM, (hh + 1) * HEAD_DIM)
        vv = slice(hh * dv, (hh + 1) * dv)
        q = q_ref[0, :, qk]
        k = k_ref[0, :, qk]
        v = v_ref[0, :, vv]
        s = lax.dot_general(q, k, (((1,), (1,)), ((), ())), preferred_element_type=F32)
        intra = jnp.dot((s * decay_ref[hh]).astype(BF16), v, preferred_element_type=F32)
        state = state_ref[hh]
        cross = jnp.dot((q.astype(F32) * xi_ref[hh]).astype(BF16), state.astype(BF16),
                        preferred_element_type=F32)
        kz = (k.astype(F32) * zeta_ref[hh]).astype(BF16)
        kv = lax.dot_general(kz, v, (((0,), (0,)), ((), ())), preferred_element_type=F32)
        state_ref[hh] = jnp.exp(log_g * blk) * state + kv
        return intra + cross

    def finish(hh, y):
        vv = slice(hh * dv, (hh + 1) * dv)
        mu = jnp.mean(y, axis=-1, keepdims=True)
        yc = y - mu
        var = jnp.mean(yc * yc, axis=-1, keepdims=True)
        gn = yc * lax.rsqrt(var + NORM_EPS) * gnw_ref[:, vv] + gnb_ref[:, vv]
        rg = rg_ref[0, :, vv].astype(F32)
        yb = rg * _sigmoid(rg) * gn
        merged = (_sigmoid(ga_ref[0, :, vv].astype(F32)) * ya_ref[0, :, vv].astype(F32)
                  + _sigmoid(gb_ref[0, :, vv].astype(F32)) * yb)
        o_ref[0, :, vv] = merged.astype(o_ref.dtype)

    ys = [mix(hh) for hh in heads]
    for hh in heads:
        finish(hh, ys[hh])


def _retention_merge(proj, ya, log_g, gn_w, gn_b, *, n_heads, d_model, col0):
    bsz, s, _ = proj.shape
    blk = RET_BLOCK
    dv = d_model // n_heads
    nw = RET_HEADS * HEAD_DIM
    ww = RET_HEADS * dv
    groups = n_heads // RET_HEADS
    q0 = col0 // nw
    k0 = q0 + groups
    v0 = (col0 + 2 * n_heads * HEAD_DIM) // ww
    rg0 = v0 + groups
    ga0 = rg0 + groups
    gb0 = ga0 + groups
    wide = lambda c0: pl.BlockSpec((1, blk, ww), lambda b, h, n, lg: (b, n, c0 + h))
    narrow = lambda c0: pl.BlockSpec((1, blk, nw), lambda b, h, n, lg: (b, n, c0 + h))
    chan = pl.BlockSpec((1, ww), lambda b, h, n, lg: (0, h))
    vmem = 2 * (2 * blk * nw * 2 + 6 * blk * ww * 2) + RET_HEADS * (blk * blk * 4 + HEAD_DIM * dv * 4) \
        + 4 * RET_HEADS * blk * blk * 4
    kernel = functools.partial(_retention_merge_kernel, blk=blk, dv=dv)
    return pl.pallas_call(
        kernel,
        grid_spec=pltpu.PrefetchScalarGridSpec(
            num_scalar_prefetch=1,
            grid=(bsz, groups, s // blk),
            in_specs=[narrow(q0), narrow(k0), wide(v0), wide(rg0), wide(ga0), wide(gb0),
                      wide(0), chan, chan],
            out_specs=wide(0),
            scratch_shapes=[pltpu.VMEM((RET_HEADS, blk, blk), F32),
                            pltpu.VMEM((RET_HEADS, HEAD_DIM, dv), F32),
                            pltpu.VMEM((RET_HEADS, blk, 1), F32), pltpu.VMEM((RET_HEADS, blk, 1), F32)]),
        out_shape=jax.ShapeDtypeStruct((bsz, s, d_model), BF16),
        compiler_params=pltpu.CompilerParams(
            dimension_semantics=("arbitrary", "arbitrary", "arbitrary"),
            vmem_limit_bytes=_vmem_limit(vmem)),
        name="retention_merge",
    )(log_g, proj, proj, proj, proj, proj, proj, ya, gn_w.reshape(1, d_model), gn_b.reshape(1, d_model))


def _outproj_kernel(m_ref, w_ref, x_ref, gate_ref, g_ref, scale_ref, shift_ref, x1_ref, h2_ref):
    half = m_ref.shape[1] // 2
    rows = [slice(0, half), slice(half, 2 * half)]
    ys = [jnp.dot(m_ref[0, r, :], w_ref[...], preferred_element_type=F32) for r in rows]
    for r, y in zip(rows, ys):
        x1 = x_ref[0, r, :] + gate_ref[0] * y
        x1_ref[0, r, :] = x1
        h2_ref[0, r, :] = _modulated_rmsnorm(x1, g_ref[...], scale_ref[0],
                                             shift_ref[0]).astype(h2_ref.dtype)


def _outproj(merged, w_bf16, x, gate, g, scale, shift):
    bsz, s, d = x.shape
    tm = OUTPROJ_TM
    rows = pl.BlockSpec((1, tm, d), lambda b, i: (b, i, 0))
    mod = pl.BlockSpec((1, 1, d), lambda b, i: (b, 0, 0))
    vmem = 2 * (tm * d * 2 + d * d * 2 + tm * d * 4 + tm * d * 4 + tm * d * 2)
    return pl.pallas_call(
        _outproj_kernel,
        grid=(bsz, s // tm),
        in_specs=[rows, pl.BlockSpec((d, d), lambda b, i: (0, 0)), rows, mod,
                  pl.BlockSpec((1, d), lambda b, i: (0, 0)), mod, mod],
        out_specs=[rows, rows],
        out_shape=[jax.ShapeDtypeStruct((bsz, s, d), F32), jax.ShapeDtypeStruct((bsz, s, d), BF16)],
        compiler_params=pltpu.CompilerParams(
            dimension_semantics=("arbitrary", "arbitrary"),
            vmem_limit_bytes=_vmem_limit(vmem)),
        name="outproj",
    )(merged, w_bf16, x, gate, g.reshape(1, d), scale, shift)


def _ffn_kernel(h_ref, wg_ref, wu_ref, wd_ref, x1_ref, gate_ref, fg_ref, o_ref, acc_ref):
    f = pl.program_id(2)

    @pl.when(f == 0)
    def _():
        acc_ref[...] = jnp.zeros_like(acc_ref)

    h = h_ref[0]
    parts = []
    for c in range(0, wg_ref.shape[1], FFN_STRIP):
        cs = slice(c, c + FFN_STRIP)
        g = jnp.dot(h, wg_ref[:, cs], preferred_element_type=F32)
        u = jnp.dot(h, wu_ref[:, cs], preferred_element_type=F32)
        parts.append((g * _sigmoid(g) * u).astype(BF16))
    a = jnp.concatenate(parts, axis=1)
    acc_ref[...] += jnp.dot(a, wd_ref[...], preferred_element_type=F32)

    @pl.when(f == pl.num_programs(2) - 1)
    def _():
        x2 = x1_ref[0] + gate_ref[0] * acc_ref[...]
        ms = jnp.mean(x2 * x2, axis=-1, keepdims=True)
        o_ref[0] = x2 * lax.rsqrt(ms + NORM_EPS) * fg_ref[...]


def _ffn(h2, wg, wu, wd, x1, gate, final_g):
    bsz, s, d = x1.shape
    dff = wg.shape[1]
    tm, tf = FFN_TM, FFN_TF
    rows = lambda: pl.BlockSpec((1, tm, d), lambda b, i, f: (b, i, 0))
    vmem = 2 * (tm * d * 2 + 2 * d * tf * 2 + tf * d * 2 + tm * d * 4 + tm * d * 4) + tm * d * 4
    return pl.pallas_call(
        _ffn_kernel,
        grid=(bsz, s // tm, dff // tf),
        in_specs=[rows(),
                  pl.BlockSpec((d, tf), lambda b, i, f: (0, f)),
                  pl.BlockSpec((d, tf), lambda b, i, f: (0, f)),
                  pl.BlockSpec((tf, d), lambda b, i, f: (f, 0)),
                  rows(),
                  pl.BlockSpec((1, 1, d), lambda b, i, f: (b, 0, 0)),
                  pl.BlockSpec((1, d), lambda b, i, f: (0, 0))],
        out_specs=rows(),
        out_shape=jax.ShapeDtypeStruct((bsz, s, d), F32),
        scratch_shapes=[pltpu.VMEM((tm, d), F32)],
        compiler_params=pltpu.CompilerParams(
            dimension_semantics=("arbitrary", "arbitrary", "arbitrary"),
            vmem_limit_bytes=_vmem_limit(vmem)),
        name="ffn",
    )(h2, wg, wu, wd, x1, gate, final_g.reshape(1, d))


def kernel(x, c, w_ada, b_ada, norm1_g, w_in, lam_q1, lam_k1, lam_q2, lam_k2, diff_subln_g,
           ret_gn_w, ret_gn_b, w_out, norm2_g, w_ffn_gate, w_ffn_up, w_ffn_down, final_g):
    bsz, s, d = x.shape
    depth = w_ada.shape[0]
    assert depth == 1, "the final RMSNorm is fused into the FFN kernel of the only layer"
    n_heads = d // (2 * HEAD_DIM)
    diff_qk_w = n_heads * 2 * HEAD_DIM
    ret_col0 = 2 * diff_qk_w + d
    slopes = jnp.asarray([2.0 ** (-8.0 * (h + 1) / n_heads) for h in range(n_heads)], F32)
    log_g = jnp.asarray([math.log(1.0 - 2.0 ** (-5 - h)) for h in range(n_heads)], F32)

    for l in range(depth):
        lam_init = 0.8 - 0.6 * math.exp(-0.3 * l)
        mod = _adaln(c, w_ada[l], b_ada[l])
        shift1, scale1, gate1, shift2, scale2, gate2 = (
            m.reshape(bsz, 1, d) for m in jnp.split(mod, 6, axis=-1))
        proj = _inproj(x, norm1_g[l], scale1, shift1, w_in[l])
        ya = _diff_attention(proj, slopes, lam_q1[l:l + 1], lam_k1[l:l + 1], lam_q2[l:l + 1],
                             lam_k2[l:l + 1], diff_subln_g[l], n_heads=n_heads, d_model=d,
                             lam_init=lam_init)
        merged = _retention_merge(proj, ya, log_g, ret_gn_w[l], ret_gn_b[l],
                                  n_heads=n_heads, d_model=d, col0=ret_col0)
        x1, h2 = _outproj(merged, w_out[l].astype(BF16), x, gate1, norm2_g[l], scale2, shift2)
        x = _ffn(h2, w_ffn_gate[l].astype(BF16), w_ffn_up[l].astype(BF16),
                 w_ffn_down[l].astype(BF16), x1, gate2, final_g)
    return x
```

```python
import functools
import math

import jax
import jax.numpy as jnp
from jax import lax
from jax.experimental import pallas as pl
from jax.experimental.pallas import tpu as pltpu

F32 = jnp.float32
BF16 = jnp.bfloat16

HEAD_DIM = 128
NORM_EPS = 1e-6
LOG2E = math.log2(math.e)

V7X_VMEM_BYTES = 64 * 1024 * 1024
V7X_SUBLANES = 8

ADA_TN = 1024
INPROJ_TM = 1024
INPROJ_TN = 1024
ATTN_BQ = 512
ATTN_BK = 512
ATTN_STRIP = 256
RET_BLOCK = 512
RET_HEADS = 4
OUTPROJ_TM = 512
FFN_TM = 512
FFN_TF = 512
FFN_STRIP = 256


def _vmem_limit(nbytes):
    return int(min(nbytes + 16 * 1024 * 1024, V7X_VMEM_BYTES - 4 * 1024 * 1024))


def _sigmoid(x):
    return 1.0 / (1.0 + jnp.exp(-x))


def _adaln_kernel(c_ref, w_ref, b_ref, o_ref):
    c = c_ref[...]
    sc = c * _sigmoid(c)
    o_ref[...] = jnp.dot(sc, w_ref[...], preferred_element_type=F32,
                         precision=lax.Precision.HIGHEST) + b_ref[...]


def _adaln(c, w, b):
    bsz, d = c.shape
    n = w.shape[1]
    rows = -(-bsz // V7X_SUBLANES) * V7X_SUBLANES
    c_pad = jnp.pad(c, ((0, rows - bsz), (0, 0)))
    out = pl.pallas_call(
        _adaln_kernel,
        grid=(n // ADA_TN,),
        in_specs=[pl.BlockSpec((rows, d), lambda j: (0, 0)),
                  pl.BlockSpec((d, ADA_TN), lambda j: (0, j)),
                  pl.BlockSpec((1, ADA_TN), lambda j: (0, j))],
        out_specs=pl.BlockSpec((rows, ADA_TN), lambda j: (0, j)),
        out_shape=jax.ShapeDtypeStruct((rows, n), F32),
        compiler_params=pltpu.CompilerParams(
            dimension_semantics=("arbitrary",),
            vmem_limit_bytes=_vmem_limit(2 * d * ADA_TN * 4)),
        name="adaln_mod",
    )(c_pad, w, b.reshape(1, n))
    return out[:bsz]


def _modulated_rmsnorm(x, g, scale, shift):
    ms = jnp.mean(x * x, axis=-1, keepdims=True)
    y = x * lax.rsqrt(ms + NORM_EPS) * g
    return y * (1.0 + scale) + shift


def _inproj_kernel(x_ref, g_ref, scale_ref, shift_ref, w_ref, o_ref, h_ref):
    @pl.when(pl.program_id(2) == 0)
    def _():
        h = _modulated_rmsnorm(x_ref[0], g_ref[...], scale_ref[0], shift_ref[0])
        h_ref[...] = h.astype(BF16)

    w = w_ref[...].astype(BF16)
    o_ref[0] = jnp.dot(h_ref[...], w, preferred_element_type=F32).astype(o_ref.dtype)


def _inproj(x, g, scale, shift, w):
    bsz, s, d = x.shape
    n = w.shape[1]
    tm, tn = INPROJ_TM, INPROJ_TN
    vmem = 2 * tm * d * 4 + 2 * d * tn * 4 + d * tn * 2 + 2 * tm * tn * 2 + tm * d * 2
    return pl.pallas_call(
        _inproj_kernel,
        grid=(bsz, s // tm, n // tn),
        in_specs=[pl.BlockSpec((1, tm, d), lambda b, i, j: (b, i, 0)),
                  pl.BlockSpec((1, d), lambda b, i, j: (0, 0)),
                  pl.BlockSpec((1, 1, d), lambda b, i, j: (b, 0, 0)),
                  pl.BlockSpec((1, 1, d), lambda b, i, j: (b, 0, 0)),
                  pl.BlockSpec((d, tn), lambda b, i, j: (0, j))],
        out_specs=pl.BlockSpec((1, tm, tn), lambda b, i, j: (b, i, j)),
        out_shape=jax.ShapeDtypeStruct((bsz, s, n), BF16),
        scratch_shapes=[pltpu.VMEM((tm, d), BF16)],
        compiler_params=pltpu.CompilerParams(
            dimension_semantics=("arbitrary", "arbitrary", "arbitrary"),
            vmem_limit_bytes=_vmem_limit(vmem)),
        name="inproj",
    )(x, g.reshape(1, d), scale, shift, w)


def _diff_attn_kernel(slopes_ref, q1_ref, q2_ref, k1_ref, k2_ref, v_ref,
                      lq1_ref, lk1_ref, lq2_ref, lk2_ref, g_ref, o_ref,
                      bias_ref, vt_ref, acc1_ref, acc2_ref, p1_ref, p2_ref, t2_ref, *, lam_init, bq, bk):
    h = pl.program_id(1)
    i = pl.program_id(2)
    slope_l2 = slopes_ref[h] * LOG2E
    n_kv = v_ref.shape[1] // bk
    ratio = bq // bk

    @pl.when(i == 0)
    def _():
        key = lax.broadcasted_iota(jnp.int32, (bk, bq), 0)
        qry = lax.broadcasted_iota(jnp.int32, (bk, bq), 1)
        bias = (key - qry).astype(F32) * slope_l2
        bias_ref[0] = bias
        for d in range(ratio):
            bias_ref[1 + d] = jnp.where(qry >= key + d * bk, bias, -jnp.inf)

        def xpose(c, _):
            start = pl.multiple_of(c * bk, bk)
            vt_ref[:, pl.ds(start, bk)] = v_ref[0, pl.ds(start, bk), :].T
            return 0
        lax.fori_loop(0, n_kv, xpose, 0)

    qscale = (HEAD_DIM ** -0.5) * LOG2E
    q1t = (q1_ref[0].astype(F32) * qscale).astype(BF16).T
    q2t = (q2_ref[0].astype(F32) * qscale).astype(BF16).T

    acc1_ref[...] = jnp.zeros_like(acc1_ref)
    acc2_ref[...] = jnp.zeros_like(acc2_ref)
    strips = [slice(c, c + ATTN_STRIP) for c in range(0, bq, ATTN_STRIP)]
    cat = lambda xs: jnp.concatenate(xs, axis=1)

    def scores(qt, k_ref, start):
        k = k_ref[0, pl.ds(start, bk), :]
        return [jnp.dot(k, qt[:, cs], preferred_element_type=F32) for cs in strips]

    def tile_offset(j):
        return slope_l2 * (i * bq - j * bk).astype(F32)

    def bias_variant(j):
        d = j - ratio * i
        return jnp.where(d >= 0, d + 1, 0)

    def softmax(ts, mxs, off, m_old, l_old, p_ref):
        m_out, l_out, a_out = [], [], []
        for cs, t, mx in zip(strips, ts, mxs):
            m_new = jnp.maximum(m_old[:, cs], mx)
            p = jnp.exp2(t() - (m_new + off))
            alpha = jnp.exp2(m_old[:, cs] - m_new)
            l_out.append(alpha * l_old[:, cs] + jnp.sum(p, axis=0, keepdims=True))
            m_out.append(m_new)
            a_out.append(alpha)
            p_ref[:, cs] = p.astype(BF16)
        return cat(m_out), cat(l_out), cat(a_out)

    def softmax_now(ss, sel, off, m_old, l_old, p_ref):
        ts = [s + bias_ref[sel, :, cs] for cs, s in zip(strips, ss)]
        mxs = [jnp.max(t, axis=0, keepdims=True) - off for t in ts]
        return softmax([lambda t=t: t for t in ts], mxs, off, m_old, l_old, p_ref)

    def park_scores(ss, sel, off):
        mxs = []
        for cs, s in zip(strips, ss):
            t = s + bias_ref[sel, :, cs]
            t2_ref[:, cs] = t
            mxs.append(jnp.max(t, axis=0, keepdims=True) - off)
        return cat(mxs)

    def softmax_parked(mx, off, m_old, l_old, p_ref):
        ts = [lambda cs=cs: t2_ref[:, cs] for cs in strips]
        return softmax(ts, [mx[:, cs] for cs in strips], off, m_old, l_old, p_ref)

    def pv(vt, acc_ref, p_ref, alpha):
        acc_ref[...] = alpha * acc_ref[...] + jnp.dot(vt, p_ref[...], preferred_element_type=F32)

    neg = jnp.full((1, bq), -jnp.inf, F32)
    zero = jnp.zeros((1, bq), F32)
    last = ratio * i + ratio - 1

    sel0 = bias_variant(0)
    off0 = tile_offset(0)
    s1 = scores(q1t, k1_ref, 0)
    s2 = scores(q2t, k2_ref, 0)
    m1, l1, a1 = softmax_now(s1, sel0, off0, neg, zero, p1_ref)
    mx2 = park_scores(s2, sel0, off0)

    def step(j, carry):
        m1, l1, a1, m2, l2, mx2 = carry
        sel = bias_variant(j)
        start = pl.multiple_of(j * bk, bk)
        vt_prev = vt_ref[:, pl.ds(pl.multiple_of((j - 1) * bk, bk), bk)]
        off = tile_offset(j)
        s1 = scores(q1t, k1_ref, start)
        pv(vt_prev, acc1_ref, p1_ref, a1)
        m2, l2, a2 = softmax_parked(mx2, tile_offset(j - 1), m2, l2, p2_ref)
        s2 = scores(q2t, k2_ref, start)
        m1, l1, a1 = softmax_now(s1, sel, off, m1, l1, p1_ref)
        pv(vt_prev, acc2_ref, p2_ref, a2)
        mx2 = park_scores(s2, sel, off)
        return m1, l1, a1, m2, l2, mx2

    m1, l1, a1, m2, l2, mx2 = lax.fori_loop(1, last + 1, step, (m1, l1, a1, neg, zero, mx2))

    vt_last = vt_ref[:, pl.ds(pl.multiple_of(last * bk, bk), bk)]
    pv(vt_last, acc1_ref, p1_ref, a1)
    m2, l2, a2 = softmax_parked(mx2, tile_offset(last), m2, l2, p2_ref)
    pv(vt_last, acc2_ref, p2_ref, a2)

    lam = (jnp.exp(jnp.sum(lq1_ref[...] * lk1_ref[...], axis=-1, keepdims=True))
           - jnp.exp(jnp.sum(lq2_ref[...] * lk2_ref[...], axis=-1, keepdims=True))
           + lam_init)
    yt = acc1_ref[...] * (1.0 / l1) - acc2_ref[...] * (lam / l2)
    ynt = yt * lax.rsqrt(jnp.mean(yt * yt, axis=0, keepdims=True) + NORM_EPS)
    o_ref[0] = (ynt * (g_ref[...] * (1.0 - lam_init))).astype(o_ref.dtype).T


def _diff_attention(proj, slopes, lq1, lk1, lq2, lk2, subln_g, *, n_heads, d_model, lam_init):
    bsz, s, _ = proj.shape
    bq, bk = ATTN_BQ, ATTN_BK
    dv = d_model // n_heads
    k_col0 = (n_heads * 2 * HEAD_DIM) // HEAD_DIM
    v_col0 = (2 * n_heads * 2 * HEAD_DIM) // dv
    qspec = lambda m: pl.BlockSpec((1, bq, HEAD_DIM), lambda b, h, i, sl: (b, i, 2 * h + m))
    kspec = lambda m: pl.BlockSpec((1, s, HEAD_DIM), lambda b, h, i, sl: (b, 0, k_col0 + 2 * h + m))
    vec = pl.BlockSpec((1, HEAD_DIM), lambda b, h, i, sl: (0, 0))
    n_bias = 1 + bq // bk
    vmem = (2 * 2 * s * HEAD_DIM * 2 + 3 * s * dv * 2 + (n_bias + 1) * bk * bq * 4 + 2 * bq * dv * 4
            + 2 * bk * bq * 2 + 6 * bk * bq * 4)
    kernel = functools.partial(_diff_attn_kernel, lam_init=lam_init, bq=bq, bk=bk)
    return pl.pallas_call(
        kernel,
        grid_spec=pltpu.PrefetchScalarGridSpec(
            num_scalar_prefetch=1,
            grid=(bsz, n_heads, s // bq),
            in_specs=[qspec(0), qspec(1), kspec(0), kspec(1),
                      pl.BlockSpec((1, s, dv), lambda b, h, i, sl: (b, 0, v_col0 + h)),
                      vec, vec, vec, vec,
                      pl.BlockSpec((dv, 1), lambda b, h, i, sl: (h, 0))],
            out_specs=pl.BlockSpec((1, bq, dv), lambda b, h, i, sl: (b, i, h)),
            scratch_shapes=[pltpu.VMEM((n_bias, bk, bq), F32),
                            pltpu.VMEM((dv, s), BF16),
                            pltpu.VMEM((dv, bq), F32), pltpu.VMEM((dv, bq), F32),
                            pltpu.VMEM((bk, bq), BF16), pltpu.VMEM((bk, bq), BF16),
                            pltpu.VMEM((bk, bq), F32)]),
        out_shape=jax.ShapeDtypeStruct((bsz, s, d_model), BF16),
        compiler_params=pltpu.CompilerParams(
            dimension_semantics=("arbitrary", "arbitrary", "arbitrary"),
            vmem_limit_bytes=_vmem_limit(vmem)),
        name="diff_attention",
    )(slopes, proj, proj, proj, proj, proj, lq1, lk1, lq2, lk2, subln_g.reshape(d_model, 1))


def _retention_merge_kernel(logg_ref, q_ref, k_ref, v_ref, rg_ref, ga_ref, gb_ref, ya_ref,
                            gnw_ref, gnb_ref, o_ref, decay_ref, state_ref, xi_ref, zeta_ref, *, blk, dv):
    hp = pl.program_id(1)
    n = pl.program_id(2)
    kscale = HEAD_DIM ** -0.5
    heads = range(RET_HEADS)
    log_gs = [logg_ref[hp * RET_HEADS + hh] for hh in heads]

    @pl.when(n == 0)
    def _():
        row = lax.broadcasted_iota(jnp.int32, (blk, blk), 0)
        col = lax.broadcasted_iota(jnp.int32, (blk, blk), 1)
        rel = (row - col).astype(F32)
        idx = lax.broadcasted_iota(jnp.int32, (blk, 1), 0).astype(F32)
        for hh in heads:
            decay_ref[hh] = jnp.where(rel >= 0, jnp.exp(log_gs[hh] * jnp.maximum(rel, 0.0)) * kscale, 0.0)
            xi_ref[hh] = jnp.exp(log_gs[hh] * (idx + 1.0))
            zeta_ref[hh] = jnp.exp(log_gs[hh] * (blk - 1.0 - idx)) * kscale
        state_ref[...] = jnp.zeros_like(state_ref)

    def mix(hh):
        log_g = log_gs[hh]
        qk = slice(hh * HEAD_DIM, (hh + 1) * HEAD_DIM)
        vv = slice(hh * dv, (hh + 1) * dv)
        q = q_ref[0, :, qk]
        k = k_ref[0, :, qk]
        v = v_ref[0, :, vv]
        s = lax.dot_general(q, k, (((1,), (1,)), ((), ())), preferred_element_type=F32)
        intra = jnp.dot((s * decay_ref[hh]).astype(BF16), v, preferred_element_type=F32)
        state = state_ref[hh]
        cross = jnp.dot((q.astype(F32) * xi_ref[hh]).astype(BF16), state.astype(BF16),
                        preferred_element_type=F32)
        kz = (k.astype(F32) * zeta_ref[hh]).astype(BF16)
        kv = lax.dot_general(kz, v, (((0,), (0,)), ((), ())), preferred_element_type=F32)
        state_ref[hh] = jnp.exp(log_g * blk) * state + kv
        return intra + cross

    def finish(hh, y):
        vv = slice(hh * dv, (hh + 1) * dv)
        mu = jnp.mean(y, axis=-1, keepdims=True)
        yc = y - mu
        var = jnp.mean(yc * yc, axis=-1, keepdims=True)
        gn = yc * lax.rsqrt(var + NORM_EPS) * gnw_ref[:, vv] + gnb_ref[:, vv]
        rg = rg_ref[0, :, vv].astype(F32)
        yb = rg * _sigmoid(rg) * gn
        merged = (_sigmoid(ga_ref[0, :, vv].astype(F32)) * ya_ref[0, :, vv].astype(F32)
                  + _sigmoid(gb_ref[0, :, vv].astype(F32)) * yb)
        o_ref[0, :, vv] = merged.astype(o_ref.dtype)

    y_prev = mix(0)
    for hh in heads[1:]:
        y = mix(hh)
        finish(hh - 1, y_prev)
        y_prev = y
    finish(RET_HEADS - 1, y_prev)


def _retention_merge(proj, ya, log_g, gn_w, gn_b, *, n_heads, d_model, col0):
    bsz, s, _ = proj.shape
    blk = RET_BLOCK
    dv = d_model // n_heads
    nw = RET_HEADS * HEAD_DIM
    ww = RET_HEADS * dv
    groups = n_heads // RET_HEADS
    q0 = col0 // nw
    k0 = q0 + groups
    v0 = (col0 + 2 * n_heads * HEAD_DIM) // ww
    rg0 = v0 + groups
    ga0 = rg0 + groups
    gb0 = ga0 + groups
    wide = lambda c0: pl.BlockSpec((1, blk, ww), lambda b, h, n, lg: (b, n, c0 + h))
    narrow = lambda c0: pl.BlockSpec((1, blk, nw), lambda b, h, n, lg: (b, n, c0 + h))
    chan = pl.BlockSpec((1, ww), lambda b, h, n, lg: (0, h))
    vmem = 2 * (2 * blk * nw * 2 + 6 * blk * ww * 2) + RET_HEADS * (blk * blk * 4 + HEAD_DIM * dv * 4) \
        + 4 * RET_HEADS * blk * blk * 4
    kernel = functools.partial(_retention_merge_kernel, blk=blk, dv=dv)
    return pl.pallas_call(
        kernel,
        grid_spec=pltpu.PrefetchScalarGridSpec(
            num_scalar_prefetch=1,
            grid=(bsz, groups, s // blk),
            in_specs=[narrow(q0), narrow(k0), wide(v0), wide(rg0), wide(ga0), wide(gb0),
                      wide(0), chan, chan],
            out_specs=wide(0),
            scratch_shapes=[pltpu.VMEM((RET_HEADS, blk, blk), F32),
                            pltpu.VMEM((RET_HEADS, HEAD_DIM, dv), F32),
                            pltpu.VMEM((RET_HEADS, blk, 1), F32), pltpu.VMEM((RET_HEADS, blk, 1), F32)]),
        out_shape=jax.ShapeDtypeStruct((bsz, s, d_model), BF16),
        compiler_params=pltpu.CompilerParams(
            dimension_semantics=("arbitrary", "arbitrary", "arbitrary"),
            vmem_limit_bytes=_vmem_limit(vmem)),
        name="retention_merge",
    )(log_g, proj, proj, proj, proj, proj, proj, ya, gn_w.reshape(1, d_model), gn_b.reshape(1, d_model))


def _outproj_kernel(m_ref, w_ref, x_ref, gate_ref, g_ref, scale_ref, shift_ref, x1_ref, h2_ref):
    half = m_ref.shape[1] // 2
    rows = [slice(0, half), slice(half, 2 * half)]
    ys = [jnp.dot(m_ref[0, r, :], w_ref[...], preferred_element_type=F32) for r in rows]
    for r, y in zip(rows, ys):
        x1 = x_ref[0, r, :] + gate_ref[0] * y
        x1_ref[0, r, :] = x1
        h2_ref[0, r, :] = _modulated_rmsnorm(x1, g_ref[...], scale_ref[0],
                                             shift_ref[0]).astype(h2_ref.dtype)


def _outproj(merged, w_bf16, x, gate, g, scale, shift):
    bsz, s, d = x.shape
    tm = OUTPROJ_TM
    rows = pl.BlockSpec((1, tm, d), lambda b, i: (b, i, 0))
    mod = pl.BlockSpec((1, 1, d), lambda b, i: (b, 0, 0))
    vmem = 2 * (tm * d * 2 + d * d * 2 + tm * d * 4 + tm * d * 4 + tm * d * 2)
    return pl.pallas_call(
        _outproj_kernel,
        grid=(bsz, s // tm),
        in_specs=[rows, pl.BlockSpec((d, d), lambda b, i: (0, 0)), rows, mod,
                  pl.BlockSpec((1, d), lambda b, i: (0, 0)), mod, mod],
        out_specs=[rows, rows],
        out_shape=[jax.ShapeDtypeStruct((bsz, s, d), F32), jax.ShapeDtypeStruct((bsz, s, d), BF16)],
        compiler_params=pltpu.CompilerParams(
            dimension_semantics=("arbitrary", "arbitrary"),
            vmem_limit_bytes=_vmem_limit(vmem)),
        name="outproj",
    )(merged, w_bf16, x, gate, g.reshape(1, d), scale, shift)


def _ffn_kernel(h_ref, wg_ref, wu_ref, wd_ref, x1_ref, gate_ref, fg_ref, o_ref, acc_ref):
    f = pl.program_id(2)

    @pl.when(f == 0)
    def _():
        acc_ref[...] = jnp.zeros_like(acc_ref)

    h = h_ref[0]
    parts = []
    for c in range(0, wg_ref.shape[1], FFN_STRIP):
        cs = slice(c, c + FFN_STRIP)
        g = jnp.dot(h, wg_ref[:, cs], preferred_element_type=F32)
        u = jnp.dot(h, wu_ref[:, cs], preferred_element_type=F32)
        parts.append((g * _sigmoid(g) * u).astype(BF16))
    a = jnp.concatenate(parts, axis=1)
    acc_ref[...] += jnp.dot(a, wd_ref[...], preferred_element_type=F32)

    @pl.when(f == pl.num_programs(2) - 1)
    def _():
        x2 = x1_ref[0] + gate_ref[0] * acc_ref[...]
        ms = jnp.mean(x2 * x2, axis=-1, keepdims=True)
        o_ref[0] = x2 * lax.rsqrt(ms + NORM_EPS) * fg_ref[...]


def _ffn(h2, wg, wu, wd, x1, gate, final_g):
    bsz, s, d = x1.shape
    dff = wg.shape[1]
    tm, tf = FFN_TM, FFN_TF
    rows = lambda: pl.BlockSpec((1, tm, d), lambda b, i, f: (b, i, 0))
    vmem = 2 * (tm * d * 2 + 2 * d * tf * 2 + tf * d * 2 + tm * d * 4 + tm * d * 4) + tm * d * 4
    return pl.pallas_call(
        _ffn_kernel,
        grid=(bsz, s // tm, dff // tf),
        in_specs=[rows(),
                  pl.BlockSpec((d, tf), lambda b, i, f: (0, f)),
                  pl.BlockSpec((d, tf), lambda b, i, f: (0, f)),
                  pl.BlockSpec((tf, d), lambda b, i, f: (f, 0)),
                  rows(),
                  pl.BlockSpec((1, 1, d), lambda b, i, f: (b, 0, 0)),
                  pl.BlockSpec((1, d), lambda b, i, f: (0, 0))],
        out_specs=rows(),
        out_shape=jax.ShapeDtypeStruct((bsz, s, d), F32),
        scratch_shapes=[pltpu.VMEM((tm, d), F32)],
        compiler_params=pltpu.CompilerParams(
            dimension_semantics=("arbitrary", "arbitrary", "arbitrary"),
            vmem_limit_bytes=_vmem_limit(vmem)),
        name="ffn",
    )(h2, wg, wu, wd, x1, gate, final_g.reshape(1, d))


def kernel(x, c, w_ada, b_ada, norm1_g, w_in, lam_q1, lam_k1, lam_q2, lam_k2, diff_subln_g,
           ret_gn_w, ret_gn_b, w_out, norm2_g, w_ffn_gate, w_ffn_up, w_ffn_down, final_g):
    bsz, s, d = x.shape
    depth = w_ada.shape[0]
    assert depth == 1, "the final RMSNorm is fused into the FFN kernel of the only layer"
    n_heads = d // (2 * HEAD_DIM)
    diff_qk_w = n_heads * 2 * HEAD_DIM
    ret_col0 = 2 * diff_qk_w + d
    slopes = jnp.asarray([2.0 ** (-8.0 * (h + 1) / n_heads) for h in range(n_heads)], F32)
    log_g = jnp.asarray([math.log(1.0 - 2.0 ** (-5 - h)) for h in range(n_heads)], F32)

    for l in range(depth):
        lam_init = 0.8 - 0.6 * math.exp(-0.3 * l)
        mod = _adaln(c, w_ada[l], b_ada[l])
        shift1, scale1, gate1, shift2, scale2, gate2 = (
            m.reshape(bsz, 1, d) for m in jnp.split(mod, 6, axis=-1))
        proj = _inproj(x, norm1_g[l], scale1, shift1, w_in[l])
        ya = _diff_attention(proj, slopes, lam_q1[l:l + 1], lam_k1[l:l + 1], lam_q2[l:l + 1],
                             lam_k2[l:l + 1], diff_subln_g[l], n_heads=n_heads, d_model=d,
                             lam_init=lam_init)
        merged = _retention_merge(proj, ya, log_g, ret_gn_w[l], ret_gn_b[l],
                                  n_heads=n_heads, d_model=d, col0=ret_col0)
        x1, h2 = _outproj(merged, w_out[l].astype(BF16), x, gate1, norm2_g[l], scale2, shift2)
        x = _ffn(h2, w_ffn_gate[l].astype(BF16), w_ffn_up[l].astype(BF16),
                 w_ffn_down[l].astype(BF16), x1, gate2, final_g)
    return x
```
